```python
import functools
import jax, jax.numpy as jnp
from jax import lax
import numpy as np

D_MODEL = 1024
BATCH = 32
SEQ = 2048
DEPTH = 1
DEC_BATCH = 128
DEC_SEQ = 1
PAST_LEN = 16384
PAGE_SIZE = 128

D_MIX = D_MODEL
D_ATT = D_MIX // 2
D_MLS = D_MIX - D_ATT
N_HEADS_A = 4
QK_NOPE = 128
QK_ROPE = 64
V_HEAD = D_ATT // N_HEADS_A
Q_LORA = (3 * D_MODEL) // 8
KV_LORA = D_MODEL // 4
ROPE_BASE = 10000.0
N_HEADS_M = 4
DH_M = D_MLS // N_HEADS_M
CONV_W = 4
MLSTM_CHUNK = 64
Q_BLOCK = 128
EPS = 1e-6
LN_EPS = 1e-5
D_IN = Q_LORA + KV_LORA + QK_ROPE + D_ATT + D_MLS + D_MLS
SPLITS = [Q_LORA, Q_LORA + KV_LORA, Q_LORA + KV_LORA + QK_ROPE,
          Q_LORA + KV_LORA + QK_ROPE + D_ATT, Q_LORA + KV_LORA + QK_ROPE + D_ATT + D_MLS]

kernel_name = "mla_mlstm_parallel_heads_step"


def rmsnorm(x, g):
    xf = x.astype(jnp.float32)
    y = xf * lax.rsqrt(jnp.mean(xf * xf, axis=-1, keepdims=True) + EPS)
    return (y * g.astype(jnp.float32)).astype(x.dtype)


def rope_tables(pos):
    inv = 1.0 / (ROPE_BASE ** (jnp.arange(0, QK_ROPE, 2, dtype=jnp.float32) / QK_ROPE))
    ang = pos.astype(jnp.float32)[:, None] * inv[None, :]
    return jnp.cos(ang), jnp.sin(ang)


def apply_rope(x, cos, sin):
    half = x.shape[-1] // 2
    x1, x2 = x[..., :half], x[..., half:]
    cos = cos.astype(x.dtype)
    sin = sin.astype(x.dtype)
    return jnp.concatenate([x1 * cos - x2 * sin, x2 * cos + x1 * sin], axis=-1)


def k_nope_from_latent(c_kv, w_uk, g_kn):
    return rmsnorm(jnp.einsum('...tr,rhd->...thd', c_kv, w_uk), g_kn)


def mla_core(q_nope, q_rope, k_nope, k_rope, c_kv, mask, w_uv):
    scale = (QK_NOPE + QK_ROPE) ** -0.5
    s = (jnp.einsum('...qhd,...khd->...hqk', q_nope, k_nope)
         + jnp.einsum('...qhd,...kd->...hqk', q_rope, k_rope)).astype(jnp.float32) * scale
    s = jnp.where(mask, s, -jnp.inf)
    p = jax.nn.softmax(s, axis=-1).astype(c_kv.dtype)
    ctx = jnp.einsum('...hqk,...kr->...qhr', p, c_kv)
    return jnp.einsum('...qhr,rhd->...qhd', ctx, w_uv)


def mla_prompt(q_nope, q_rope, c_kv, k_rope, w_uk, g_kn, w_uv):
    B, S = q_nope.shape[:2]
    k_nope = k_nope_from_latent(c_kv, w_uk, g_kn)
    kpos = jnp.arange(S)

    def block(i):
        start = i * Q_BLOCK
        qn = lax.dynamic_slice_in_dim(q_nope, start, Q_BLOCK, axis=1)
        qr = lax.dynamic_slice_in_dim(q_rope, start, Q_BLOCK, axis=1)
        qpos = start + jnp.arange(Q_BLOCK)
        mask = kpos[None, :] <= qpos[:, None]
        return mla_core(qn, qr, k_nope, k_rope, c_kv, mask, w_uv)

    out = lax.map(block, jnp.arange(S // Q_BLOCK))
    return jnp.moveaxis(out, 0, 1).reshape(B, S, N_HEADS_A * V_HEAD)


def mla_sample(q_nope, q_rope, c_kv, k_rope, w_uk, g_kn, w_uv, cache_ckv, cache_krope, page_table):
    Bd, Sd = q_nope.shape[:2]
    kidx = jnp.arange(PAST_LEN + Sd)
    qidx = PAST_LEN + jnp.arange(Sd)
    mask = kidx[None, :] <= qidx[:, None]

    def one(args):
        pt, qn, qr, cn, krn = args
        c_all = jnp.concatenate([cache_ckv[pt].reshape(-1, KV_LORA), cn], axis=0)
        kr_all = jnp.concatenate([cache_krope[pt].reshape(-1, QK_ROPE), krn], axis=0)
        k_nope = k_nope_from_latent(c_all, w_uk, g_kn)
        return mla_core(qn, qr, k_nope, kr_all, c_all, mask, w_uv)

    out = lax.map(one, (page_table, q_nope, q_rope, c_kv, k_rope))
    return out.reshape(Bd, Sd, N_HEADS_A * V_HEAD)


def causal_conv(u, buf, conv_w, conv_b):
    S = u.shape[1]
    full = jnp.concatenate([buf.astype(u.dtype), u], axis=1)
    y = conv_b + full[:, 0:S] * conv_w[0]
    for j in range(1, CONV_W):
        y = y + full[:, j:j + S] * conv_w[j]
    return y, full[:, -(CONV_W - 1):]


def mlstm_chunked(q, k, v, log_i, log_f, C0, n0, m0, chunk):
    B, H, S, d = q.shape
    nc = S // chunk

    def to_chunks(a):
        return jnp.moveaxis(a.reshape(B, H, nc, chunk, *a.shape[3:]), 2, 0)

    causal = jnp.tril(jnp.ones((chunk, chunk), dtype=bool))

    def step(carry, xs):
        C, n, m = carry
        qc, kc, vc, li, lf = xs
        b = jnp.cumsum(lf, axis=-1)
        a = b + m[..., None]
        D = jnp.where(causal, b[..., :, None] - b[..., None, :] + li[..., None, :], -jnp.inf)
        mt = jnp.maximum(a, jnp.max(D, axis=-1))
        w_inter = jnp.exp(a - mt)
        W = jnp.exp(D - mt[..., None])
        qk = jnp.einsum('bhtd,bhsd->bhts', qc, kc) * W
        num = w_inter[..., None] * jnp.einsum('bhtd,bhde->bhte', qc, C) + jnp.einsum('bhts,bhse->bhte', qk, vc)
        den = w_inter * jnp.einsum('bhtd,bhd->bht', qc, n) + jnp.sum(qk, axis=-1)
        h = num / jnp.maximum(jnp.abs(den), jnp.exp(-mt))[..., None]
        m_new = mt[..., -1]
        w_s = jnp.exp(b[..., -1:] - b + li - m_new[..., None])
        decay = jnp.exp(a[..., -1] - m_new)
        C_new = decay[..., None, None] * C + jnp.einsum('bhs,bhsd,bhse->bhde', w_s, kc, vc)
        n_new = decay[..., None] * n + jnp.einsum('bhs,bhsd->bhd', w_s, kc)
        return (C_new, n_new, m_new), h

    xs = (to_chunks(q), to_chunks(k), to_chunks(v), to_chunks(log_i), to_chunks(log_f))
    (C, n, m), h = lax.scan(step, (C0, n0, m0), xs)
    h = jnp.moveaxis(h, 0, 2).reshape(B, H, S, -1)
    return h, C, n, m


def head_layernorm(h, g):
    B, S = h.shape[:2]
    hf = h.astype(jnp.float32)
    mu = jnp.mean(hf, axis=-1, keepdims=True)
    var = jnp.mean(jnp.square(hf - mu), axis=-1, keepdims=True)
    y = (hf - mu) * lax.rsqrt(var + LN_EPS)
    return y.reshape(B, S, -1) * g.astype(jnp.float32)


def mlstm_branch(m_in, conv_buf, C0, n0, m0, chunk, conv_w, conv_b, w_q_m, w_k_m, w_v_m,
                 w_gate, b_i, b_f, mh_norm_g, skip):
    B, S, _ = m_in.shape
    f32 = jnp.float32
    conv, new_buf = causal_conv(m_in, conv_buf, conv_w, conv_b)
    ca = jax.nn.silu(conv)
    q = jnp.einsum('bshd,hde->bshe', ca.reshape(B, S, N_HEADS_M, DH_M), w_q_m)
    k = jnp.einsum('bshd,hde->bshe', ca.reshape(B, S, N_HEADS_M, DH_M), w_k_m)
    v = jnp.einsum('bshd,hde->bshe', m_in.reshape(B, S, N_HEADS_M, DH_M), w_v_m)
    qkv = jnp.concatenate([q.reshape(B, S, -1), k.reshape(B, S, -1), v.reshape(B, S, -1)], axis=-1)
    gates = jnp.einsum('bse,eg->bsg', qkv, w_gate).astype(f32)
    log_i = jnp.swapaxes(gates[..., :N_HEADS_M] + b_i.astype(f32), 1, 2)
    log_f = jnp.swapaxes(jax.nn.log_sigmoid(gates[..., N_HEADS_M:] + b_f.astype(f32)), 1, 2)
    to_bhsd = lambda a: jnp.swapaxes(a.astype(f32), 1, 2)
    h, C, n, m = mlstm_chunked(to_bhsd(q), to_bhsd(k) * (DH_M ** -0.5), to_bhsd(v), log_i, log_f,
                               C0.astype(f32), n0.astype(f32), m0.astype(f32), chunk)
    h = head_layernorm(jnp.swapaxes(h, 1, 2), mh_norm_g)
    out = (h + skip.astype(f32) * ca.astype(f32)).astype(m_in.dtype)
    return out, new_buf, C, n, m


def hybrid_layer(x, pos, conv_buf, C0, n0, m0, chunk, attend, W):
    (norm_g, w_in, q_norm_g, w_uq, kv_norm_g, w_uk, w_uv, g_qn, g_qr, g_kn, g_kr,
     conv_w, conv_b, w_q_m, w_k_m, w_v_m, w_gate, b_i, b_f, mh_norm_g, skip, w_out) = W
    B, S, _ = x.shape
    h = rmsnorm(x, norm_g)
    p = jnp.einsum('bsd,de->bse', h, w_in)
    q_lat, kv_lat, kr_raw, z_a, m_in, z_m = jnp.split(p, SPLITS, axis=-1)
    q = jnp.einsum('bsr,re->bse', rmsnorm(q_lat, q_norm_g), w_uq).reshape(B, S, N_HEADS_A, QK_NOPE + QK_ROPE)
    cos, sin = rope_tables(pos)
    q_nope = rmsnorm(q[..., :QK_NOPE], g_qn)
    q_rope = apply_rope(rmsnorm(q[..., QK_NOPE:], g_qr), cos[:, None, :], sin[:, None, :])
    c_kv = rmsnorm(kv_lat, kv_norm_g)
    k_rope = apply_rope(rmsnorm(kr_raw, g_kr), cos, sin)
    att = attend(q_nope, q_rope, c_kv, k_rope, w_uk, g_kn, w_uv)
    mls, new_buf, C, n, m = mlstm_branch(m_in, conv_buf, C0, n0, m0, chunk, conv_w, conv_b,
                                         w_q_m, w_k_m, w_v_m, w_gate, b_i, b_f, mh_norm_g, skip)
    mixed = jnp.concatenate([att * jax.nn.silu(z_a), mls * jax.nn.silu(z_m)], axis=-1)
    y = x + jnp.einsum('bse,ed->bsd', mixed, w_out)
    return y, (c_kv, k_rope, C, n, m, new_buf)


def setup_inputs(seed: int = 0) -> dict:
    key = jax.random.key(seed)
    ks = jax.random.split(key, 40)
    f32 = jnp.float32
    N_PAGES = PAST_LEN // PAGE_SIZE
    n_used = DEC_BATCH * N_PAGES
    N_POOL = n_used + n_used // 4 + 1
    nrm = lambda k, shape, s: jax.random.normal(k, shape, f32) * s
    gain = lambda k, shape: 1.0 + 0.01 * jax.random.normal(k, shape, f32)
    L = DEPTH
    perm = jax.random.permutation(ks[0], N_POOL)[:n_used]
    page_table = perm.reshape(DEC_BATCH, N_PAGES).astype(jnp.int32)
    b_f = jnp.linspace(3.0, 6.0, N_HEADS_M, dtype=f32)[None, :] + nrm(ks[31], (L, N_HEADS_M), 0.01)
    return {
        "x_prompt": nrm(ks[1], (BATCH, SEQ, D_MODEL), 1.0),
        "x_sample": nrm(ks[2], (DEC_BATCH, DEC_SEQ, D_MODEL), 1.0),
        "cache_ckv": nrm(ks[3], (L, N_POOL, PAGE_SIZE, KV_LORA), 1.0),
        "cache_krope": nrm(ks[4], (L, N_POOL, PAGE_SIZE, QK_ROPE), 1.0),
        "state_C": nrm(ks[5], (L, DEC_BATCH, N_HEADS_M, DH_M, DH_M), 0.1),
        "state_n": nrm(ks[6], (L, DEC_BATCH, N_HEADS_M, DH_M), 0.1),
        "state_m": nrm(ks[7], (L, DEC_BATCH, N_HEADS_M), 0.5),
        "state_conv": nrm(ks[8], (L, DEC_BATCH, CONV_W - 1, D_MLS), 1.0),
        "page_table": page_table,
        "norm_g": gain(ks[9], (L, D_MODEL)),
        "w_in": nrm(ks[10], (L, D_MODEL, D_IN), D_MODEL ** -0.5),
        "q_norm_g": gain(ks[11], (L, Q_LORA)),
        "w_uq": nrm(ks[12], (L, Q_LORA, N_HEADS_A * (QK_NOPE + QK_ROPE)), Q_LORA ** -0.5),
        "kv_norm_g": gain(ks[13], (L, KV_LORA)),
        "w_uk": nrm(ks[14], (L, KV_LORA, N_HEADS_A, QK_NOPE), KV_LORA ** -0.5),
        "w_uv": nrm(ks[15], (L, KV_LORA, N_HEADS_A, V_HEAD), KV_LORA ** -0.5),
        "g_qn": gain(ks[16], (L, QK_NOPE)),
        "g_qr": gain(ks[17], (L, QK_ROPE)),
        "g_kn": gain(ks[18], (L, QK_NOPE)),
        "g_kr": gain(ks[19], (L, QK_ROPE)),
        "conv_w": nrm(ks[20], (L, CONV_W, D_MLS), CONV_W ** -0.5),
        "conv_b": nrm(ks[21], (L, D_MLS), 0.01),
        "w_q_m": nrm(ks[22], (L, N_HEADS_M, DH_M, DH_M), DH_M ** -0.5),
        "w_k_m": nrm(ks[23], (L, N_HEADS_M, DH_M, DH_M), DH_M ** -0.5),
        "w_v_m": nrm(ks[24], (L, N_HEADS_M, DH_M, DH_M), DH_M ** -0.5),
        "w_gate": nrm(ks[25], (L, 3 * D_MLS, 2 * N_HEADS_M), 0.1 * (3 * D_MLS) ** -0.5),
        "b_i": nrm(ks[26], (L, N_HEADS_M), 0.1),
        "b_f": b_f,
        "mh_norm_g": gain(ks[27], (L, D_MLS)),
        "skip": gain(ks[28], (L, D_MLS)),
        "w_out": nrm(ks[29], (L, D_MIX, D_MODEL), D_MIX ** -0.5),
    }


def reference(x_prompt, x_sample, cache_ckv, cache_krope, state_C, state_n, state_m, state_conv,
              page_table, norm_g, w_in, q_norm_g, w_uq, kv_norm_g, w_uk, w_uv, g_qn, g_qr, g_kn, g_kr,
              conv_w, conv_b, w_q_m, w_k_m, w_v_m, w_gate, b_i, b_f, mh_norm_g, skip, w_out):
    B, S = x_prompt.shape[:2]
    Bd, Sd = x_sample.shape[:2]
    pos_p = jnp.arange(S)
    pos_s = PAST_LEN + jnp.arange(Sd)
    yp, ys = x_prompt, x_sample
    outs_p, outs_s = [], []
    for l in range(DEPTH):
        W = (norm_g[l], w_in[l], q_norm_g[l], w_uq[l], kv_norm_g[l], w_uk[l], w_uv[l],
             g_qn[l], g_qr[l], g_kn[l], g_kr[l], conv_w[l], conv_b[l], w_q_m[l], w_k_m[l], w_v_m[l],
             w_gate[l], b_i[l], b_f[l], mh_norm_g[l], skip[l], w_out[l])
        yp, sp = hybrid_layer(
            yp, pos_p,
            jnp.zeros((B, CONV_W - 1, D_MLS), yp.dtype),
            jnp.zeros((B, N_HEADS_M, DH_M, DH_M), jnp.float32),
            jnp.zeros((B, N_HEADS_M, DH_M), jnp.float32),
            jnp.zeros((B, N_HEADS_M), jnp.float32),
            MLSTM_CHUNK, mla_prompt, W)
        attend_s = functools.partial(mla_sample, cache_ckv=cache_ckv[l], cache_krope=cache_krope[l],
                                     page_table=page_table)
        ys, ss = hybrid_layer(ys, pos_s, state_conv[l], state_C[l], state_n[l], state_m[l],
                              Sd, attend_s, W)
        outs_p.append(sp)
        outs_s.append(ss)
    stk = lambda outs, i: jnp.stack([o[i] for o in outs], axis=0)
    return (yp, ys,
            stk(outs_p, 0), stk(outs_p, 1), stk(outs_p, 2), stk(outs_p, 3), stk(outs_p, 4), stk(outs_p, 5),
            stk(outs_s, 0), stk(outs_s, 1), stk(outs_s, 2), stk(outs_s, 3), stk(outs_s, 4), stk(outs_s, 5))
```

```python
import functools

import jax
import jax.numpy as jnp
from jax import lax
from jax.experimental import pallas as pl
from jax.experimental.pallas import tpu as pltpu

f32 = jnp.float32
bf16 = jnp.bfloat16

D_MODEL = 1024
D_ATT = 512
D_MLS = 512
N_HEADS_A = 4
QK_NOPE = 128
QK_ROPE = 64
V_HEAD = 128
Q_LORA = 384
KV_LORA = 256
ROPE_BASE = 10000.0
N_HEADS_M = 4
DH_M = 128
CONV_W = 4
PAGE_SIZE = 128
EPS = 1e-6
LN_EPS = 1e-5
ATT_SCALE = (QK_NOPE + QK_ROPE) ** -0.5
K_SCALE_M = DH_M ** -0.5

LANES = 128
QK_PAD = 256
VMEM_LIMIT = 56 * 1024 * 1024

NT_DIMS = (((1,), (1,)), ((), ()))


def _rms(x, n):
    ms = jnp.sum(x * x, axis=-1, keepdims=True) * (1.0 / n)
    return x * lax.rsqrt(ms + EPS)


def _silu(x):
    return x * jax.nn.sigmoid(x)


def _dot(a, b):
    return jnp.dot(a, b, preferred_element_type=f32)


def _dot_nt(a, b):
    return lax.dot_general(a, b, NT_DIMS, preferred_element_type=f32)


def _rep_lanes(x, n):
    return x if n == 1 else jnp.concatenate([x] * n, axis=1)


def _rope_table_kernel(ang_ref, cos_ref, sin_ref):
    ang = ang_ref[...]
    lane = lax.broadcasted_iota(jnp.int32, ang.shape, 1)
    cos_ref[...] = jnp.cos(ang)
    s = jnp.sin(ang)
    sin_ref[...] = jnp.where((lane % QK_ROPE) < QK_ROPE // 2, -s, s)


def _rope_tables(pos):
    half = QK_ROPE // 2
    inv = 1.0 / (ROPE_BASE ** (jnp.arange(0, QK_ROPE, 2, dtype=f32) / QK_ROPE))
    ang = pos.astype(f32)[:, None] * inv[None, :]
    ang = jnp.tile(ang, (1, LANES // half))
    s = ang.shape[0]
    return pl.pallas_call(
        _rope_table_kernel,
        out_shape=(jax.ShapeDtypeStruct((s, LANES), f32), jax.ShapeDtypeStruct((s, LANES), f32)),
        name="rope_tables",
    )(ang)


def _proj_kernel(x_ref, cos_ref, sin_ref, ng_ref, w1_ref, w2_ref, qg_ref, wuq_ref, kvg_ref, wuk_ref,
                 gqn_ref, gqr_ref, gkn_ref, gkr_ref,
                 q_ref, k_ref, ckv_ref, ckvb_ref, kr_ref, sza_ref, min_ref, szm_ref):
    x = x_ref[0]
    tm = x.shape[0]
    h = (_rms(x, D_MODEL) * ng_ref[...]).astype(bf16)
    p1 = _dot(h, w1_ref[...])
    p2 = _dot(h, w2_ref[...])

    cos = cos_ref[...]
    sin = sin_ref[...]
    lane = lax.broadcasted_iota(jnp.int32, (tm, LANES), 1)
    first_half = (lane % QK_ROPE) < QK_ROPE // 2

    def rope(xp):
        sw = jnp.where(first_half, pltpu.roll(xp, LANES - QK_ROPE // 2, 1), pltpu.roll(xp, QK_ROPE // 2, 1))
        return xp * cos + sw * sin

    ql = (_rms(p1[:, :Q_LORA], Q_LORA) * qg_ref[...]).astype(bf16)
    qf = _dot(ql, wuq_ref[...])
    for hh in range(N_HEADS_A):
        qn = _rms(qf[:, LANES * hh:LANES * (hh + 1)], QK_NOPE) * gqn_ref[...]
        q_ref[0, hh, :, 0:LANES] = (qn * ATT_SCALE).astype(bf16)
        o = N_HEADS_A * QK_NOPE + LANES * hh
        qr = rope(_rms(qf[:, o:o + LANES], QK_ROPE) * gqr_ref[...])
        q_ref[0, hh, :, LANES:QK_PAD] = (qr * ATT_SCALE).astype(bf16)

    c = _rms(p1[:, Q_LORA:Q_LORA + KV_LORA], KV_LORA) * kvg_ref[...]
    ckv_ref[0] = c
    cb = c.astype(bf16)
    ckvb_ref[0] = cb
    kn = _dot(cb, wuk_ref[...])
    o = Q_LORA + KV_LORA
    krp = rope(_rms(p1[:, o:o + LANES], QK_ROPE) * gkr_ref[...])
    kr_ref[0] = krp[:, :QK_ROPE]
    krb = krp.astype(bf16)
    for hh in range(N_HEADS_A):
        knh = _rms(kn[:, LANES * hh:LANES * (hh + 1)], QK_NOPE) * gkn_ref[...]
        k_ref[0, hh, :, 0:LANES] = knh.astype(bf16)
        k_ref[0, hh, :, LANES:QK_PAD] = krb

    sza_ref[0] = _silu(p2[:, :D_ATT]).astype(bf16)
    min_ref[0] = p2[:, D_ATT:D_ATT + D_MLS]
    szm_ref[0] = _silu(p2[:, D_ATT + D_MLS:]).astype(bf16)


def _proj(x, cos, sin, wp, tm):
    nb, s, _ = x.shape
    grid = (nb, s // tm)
    full = lambda a: pl.BlockSpec(a.shape, lambda b, i: (0,) * a.ndim)
    tok = lambda w: pl.BlockSpec((1, tm, w), lambda b, i: (b, i, 0))
    head = pl.BlockSpec((1, N_HEADS_A, tm, QK_PAD), lambda b, i: (b, 0, i, 0))
    tab = pl.BlockSpec((tm, LANES), lambda b, i: (i, 0))
    weights = (wp["norm_g"], wp["w1"], wp["w2"], wp["q_norm_g"], wp["w_uq"], wp["kv_norm_g"], wp["w_uk"],
               wp["g_qn"], wp["g_qr"], wp["g_kn"], wp["g_kr"])
    out_shape = (
        jax.ShapeDtypeStruct((nb, N_HEADS_A, s, QK_PAD), bf16),
        jax.ShapeDtypeStruct((nb, N_HEADS_A, s, QK_PAD), bf16),
        jax.ShapeDtypeStruct((nb, s, KV_LORA), f32),
        jax.ShapeDtypeStruct((nb, s, KV_LORA), bf16),
        jax.ShapeDtypeStruct((nb, s, QK_ROPE), f32),
        jax.ShapeDtypeStruct((nb, s, D_ATT), bf16),
        jax.ShapeDtypeStruct((nb, s, D_MLS), f32),
        jax.ShapeDtypeStruct((nb, s, D_MLS), bf16),
    )
    out_specs = (head, head, tok(KV_LORA), tok(KV_LORA), tok(QK_ROPE), tok(D_ATT), tok(D_MLS), tok(D_MLS))
    return pl.pallas_call(
        _proj_kernel,
        grid=grid,
        in_specs=[tok(D_MODEL), tab, tab] + [full(w) for w in weights],
        out_specs=out_specs,
        out_shape=out_shape,
        compiler_params=pltpu.CompilerParams(dimension_semantics=("arbitrary", "arbitrary"),
                                             vmem_limit_bytes=VMEM_LIMIT),
        name="in_proj",
    )(x, cos, sin, *weights)


def _attn_kernel(q_ref, k_ref, c_ref, sza_ref, wuv_ref, o_ref, m_sc, l_sc, acc_sc, *, t):
    i = pl.program_id(1)
    nrep = t // LANES
    row = lax.broadcasted_iota(jnp.int32, (t, t), 0)
    col = lax.broadcasted_iota(jnp.int32, (t, t), 1)
    causal = col <= row

    for hh in range(N_HEADS_A):
        qh = q_ref[0, hh]
        m_sc[...] = jnp.full(m_sc.shape, -jnp.inf, f32)
        l_sc[...] = jnp.zeros(l_sc.shape, f32)
        acc_sc[...] = jnp.zeros(acc_sc.shape, f32)

        def step(j, masked):
            start = pl.multiple_of(j * t, t)
            s = _dot_nt(qh, k_ref[0, hh, pl.ds(start, t), :])
            if masked:
                s = jnp.where(causal, s, -jnp.inf)
            m_old = m_sc[...]
            m_new = jnp.maximum(m_old, jnp.max(s, axis=1, keepdims=True))
            alpha = jnp.exp(m_old - m_new)
            p = jnp.exp(s - _rep_lanes(m_new, nrep))
            l_sc[...] = alpha * l_sc[...] + jnp.sum(p, axis=1, keepdims=True)
            pv = _dot(p.astype(bf16), c_ref[0, pl.ds(start, t), :])
            acc_sc[...] = _rep_lanes(alpha, KV_LORA // LANES) * acc_sc[...] + pv
            m_sc[...] = m_new

        def body(j, carry):
            step(j, False)
            return carry

        lax.fori_loop(0, i, body, 0)
        step(i, True)

        inv = 1.0 / l_sc[...]
        ctx = (acc_sc[...] * _rep_lanes(inv, KV_LORA // LANES)).astype(bf16)
        att = _dot(ctx, wuv_ref[hh])
        sl = slice(V_HEAD * hh, V_HEAD * (hh + 1))
        o_ref[0, :, sl] = (att * sza_ref[0, :, sl].astype(f32)).astype(bf16)


def _attn_prompt(q, k, cb, sza, wuv, t):
    nb, _, s, _ = q.shape
    return pl.pallas_call(
        functools.partial(_attn_kernel, t=t),
        grid=(nb, s // t),
        in_specs=[
            pl.BlockSpec((1, N_HEADS_A, t, QK_PAD), lambda b, i: (b, 0, i, 0)),
            pl.BlockSpec((1, N_HEADS_A, s, QK_PAD), lambda b, i: (b, 0, 0, 0)),
            pl.BlockSpec((1, s, KV_LORA), lambda b, i: (b, 0, 0)),
            pl.BlockSpec((1, t, D_ATT), lambda b, i: (b, i, 0)),
            pl.BlockSpec(wuv.shape, lambda b, i: (0, 0, 0)),
        ],
        out_specs=pl.BlockSpec((1, t, D_ATT), lambda b, i: (b, i, 0)),
        out_shape=jax.ShapeDtypeStruct((nb, s, D_ATT), bf16),
        scratch_shapes=[pltpu.VMEM((t, LANES), f32), pltpu.VMEM((t, LANES), f32), pltpu.VMEM((t, KV_LORA), f32)],
        compiler_params=pltpu.CompilerParams(dimension_semantics=("arbitrary", "arbitrary"),
                                             vmem_limit_bytes=VMEM_LIMIT),
        name="attn_prompt",
    )(q, k, cb, sza, wuv)


def _scan_lanes(x, op, fill, length):
    lane = lax.broadcasted_iota(jnp.int32, x.shape, 1)
    d = 1
    while d < length:
        x = op(x, jnp.where(lane >= d, pltpu.roll(x, d, 1), fill))
        d *= 2
    return x


def _mlstm_kernel(min_ref, szm_ref, cw_ref, cb_ref, wqk_ref, wv_ref, wg_ref, gb_ref, lng_ref, skip_ref,
                  mix_ref, cn_ref, m_ref, halo_ref,
                  xbuf, ca_sc, qkv_sc, ks_sc, cn_sc, m_sc, *, ts, lc):
    si = pl.program_id(1)
    ns = pl.num_programs(1)
    nc = ts // lc

    @pl.when(si == 0)
    def _():
        xbuf[0:8, :] = jnp.zeros((8, D_MLS), f32)
        cn_sc[...] = jnp.zeros(cn_sc.shape, f32)
        m_sc[...] = jnp.zeros(m_sc.shape, f32)

    x = min_ref[0]
    xbuf[8:8 + ts, :] = x
    y = cb_ref[...] + xbuf[8:8 + ts, :] * cw_ref[3:4, :]
    for j in range(CONV_W - 1):
        y = y + xbuf[5 + j:5 + j + ts, :] * cw_ref[j:j + 1, :]
    xbuf[0:8, :] = xbuf[ts:ts + 8, :]
    ca = _silu(y)
    ca_sc[...] = ca
    cab = ca.astype(bf16)
    xb = x.astype(bf16)
    for hh in range(N_HEADS_M):
        sl = slice(DH_M * hh, DH_M * (hh + 1))
        qk = _dot(cab[:, sl], wqk_ref[hh])
        qkv_sc[:, sl] = qk[:, :DH_M].astype(bf16)
        kh = qk[:, DH_M:]
        qkv_sc[:, D_MLS + DH_M * hh:D_MLS + DH_M * (hh + 1)] = kh.astype(bf16)
        ks_sc[:, sl] = (kh * K_SCALE_M).astype(bf16)
        qkv_sc[:, 2 * D_MLS + DH_M * hh:2 * D_MLS + DH_M * (hh + 1)] = _dot(xb[:, sl], wv_ref[hh]).astype(bf16)

    g = _dot(qkv_sc[...], wg_ref[...]) + gb_ref[...]
    gt = g.T[0:8, :]
    li_all = gt
    lf_all = jax.nn.log_sigmoid(pltpu.roll(gt, 4, 0))

    row = lax.broadcasted_iota(jnp.int32, (lc, lc), 0)
    col = lax.broadcasted_iota(jnp.int32, (lc, lc), 1)
    causal = col <= row
    ones_b = jnp.ones((lc, LANES), bf16)

    for c in range(nc):
        cs = slice(lc * c, lc * (c + 1))
        li = li_all[:, cs]
        lf = lf_all[:, cs]
        m0 = m_sc[...]
        b = _scan_lanes(lf, jnp.add, 0.0, lc)
        u = li - b
        gmax = jnp.maximum(_rep_lanes(m0, lc // LANES), _scan_lanes(u, jnp.maximum, -jnp.inf, lc))
        mt = b + gmax
        w_inter = jnp.exp(_rep_lanes(m0, lc // LANES) - gmax)
        emt = jnp.exp(-mt)
        g_last = jnp.maximum(m0, jnp.max(u, axis=1, keepdims=True))
        m_new = jnp.sum(lf, axis=1, keepdims=True) + g_last
        decay = jnp.exp(m0 - g_last)
        w_s = jnp.exp(u - _rep_lanes(g_last, lc // LANES))
        stack = jnp.concatenate([gmax, w_inter, emt, jnp.zeros((lc - 24, lc), f32)], axis=0)
        cols = stack.T

        for hh in range(N_HEADS_M):
            sl = slice(DH_M * hh, DH_M * (hh + 1))
            qc = qkv_sc[cs, sl]
            kc = ks_sc[cs, sl]
            vc = qkv_sc[cs, 2 * D_MLS + DH_M * hh:2 * D_MLS + DH_M * (hh + 1)]
            s_qk = _dot_nt(qc, kc)
            wm = jnp.where(causal, jnp.exp(u[hh:hh + 1, :] - cols[:, hh:hh + 1]), 0.0)
            p = (s_qk * wm).astype(bf16)
            v_aug = jnp.concatenate([vc, ones_b], axis=1)
            intra = _dot(p, v_aug)
            inter = _dot(qc, cn_sc[hh].astype(bf16))
            tot = cols[:, 8 + hh:9 + hh] * inter + intra
            num = tot[:, :DH_M]
            den = tot[:, DH_M:]
            hv = num / jnp.maximum(jnp.abs(den), cols[:, 16 + hh:17 + hh])
            mu = jnp.mean(hv, axis=-1, keepdims=True)
            hc = hv - mu
            var = jnp.mean(hc * hc, axis=-1, keepdims=True)
            hn = hc * lax.rsqrt(var + LN_EPS) * lng_ref[:, sl]
            out = hn + skip_ref[:, sl] * ca_sc[cs, sl]
            mix_ref[0, cs, sl] = (out * szm_ref[0, cs, sl].astype(f32)).astype(bf16)
            ktw = (kc.astype(f32).T * w_s[hh:hh + 1, :]).astype(bf16)
            upd = _dot(ktw, v_aug)
            dec = jnp.broadcast_to(decay[hh:hh + 1, :], (DH_M, LANES))
            cn_sc[hh] = _rep_lanes(dec, 2) * cn_sc[hh] + upd
        m_sc[...] = m_new

    @pl.when(si == ns - 1)
    def _():
        cn_ref[0] = cn_sc[...]
        m_ref[0] = m_sc[...]
        halo_ref[0] = xbuf[0:8, :]


def _mlstm_prompt(m_in, szm, wp, ts, lc):
    nb, s, _ = m_in.shape
    full = lambda a: pl.BlockSpec(a.shape, lambda b, i: (0,) * a.ndim)
    tok = pl.BlockSpec((1, ts, D_MLS), lambda b, i: (b, i, 0))
    weights = (wp["conv_w"], wp["conv_b"], wp["w_qk_m"], wp["w_v_m"], wp["w_gate"], wp["gate_b"],
               wp["mh_norm_g"], wp["skip"])
    return pl.pallas_call(
        functools.partial(_mlstm_kernel, ts=ts, lc=lc),
        grid=(nb, s // ts),
        in_specs=[tok, tok] + [full(w) for w in weights],
        out_specs=(tok,
                   pl.BlockSpec((1, N_HEADS_M, DH_M, 2 * DH_M), lambda b, i: (b, 0, 0, 0)),
                   pl.BlockSpec((1, 8, LANES), lambda b, i: (b, 0, 0)),
                   pl.BlockSpec((1, 8, D_MLS), lambda b, i: (b, 0, 0))),
        out_shape=(jax.ShapeDtypeStruct((nb, s, D_MLS), bf16),
                   jax.ShapeDtypeStruct((nb, N_HEADS_M, DH_M, 2 * DH_M), f32),
                   jax.ShapeDtypeStruct((nb, 8, LANES), f32),
                   jax.ShapeDtypeStruct((nb, 8, D_MLS), f32)),
        scratch_shapes=[pltpu.VMEM((ts + 8, D_MLS), f32),
                        pltpu.VMEM((ts, D_MLS), f32),
                        pltpu.VMEM((ts, 3 * D_MLS), bf16),
                        pltpu.VMEM((ts, D_MLS), bf16),
                        pltpu.VMEM((N_HEADS_M, DH_M, 2 * DH_M), f32),
                        pltpu.VMEM((8, LANES), f32)],
        compiler_params=pltpu.CompilerParams(dimension_semantics=("arbitrary", "arbitrary"),
                                             vmem_limit_bytes=VMEM_LIMIT),
        name="mlstm_prompt",
    )(m_in, szm, *weights)


def _out_kernel(x_ref, ma_ref, mm_ref, wa_ref, wm_ref, y_ref):
    y_ref[...] = x_ref[...] + _dot(ma_ref[...], wa_ref[...]) + _dot(mm_ref[...], wm_ref[...])


def _out_proj(x2, mixa, mixm, wp, tm):
    n = x2.shape[0]
    full = lambda a: pl.BlockSpec(a.shape, lambda i: (0,) * a.ndim)
    return pl.pallas_call(
        _out_kernel,
        grid=(n // tm,),
        in_specs=[pl.BlockSpec((tm, D_MODEL), lambda i: (i, 0)),
                  pl.BlockSpec((tm, D_ATT), lambda i: (i, 0)),
                  pl.BlockSpec((tm, D_MLS), lambda i: (i, 0)),
                  full(wp["w_out_a"]), full(wp["w_out_m"])],
        out_specs=pl.BlockSpec((tm, D_MODEL), lambda i: (i, 0)),
        out_shape=jax.ShapeDtypeStruct((n, D_MODEL), f32),
        compiler_params=pltpu.CompilerParams(dimension_semantics=("arbitrary",), vmem_limit_bytes=VMEM_LIMIT),
        name="out_proj",
    )(x2, mixa, mixm, wp["w_out_a"], wp["w_out_m"])


def _sattn_kernel(pt_ref, q_ref, knew_ref, cnew_ref, wukt_ref, gkn_ref, ckv_hbm, kr_hbm, ctx_ref,
                  cbuf, kbuf, sem, cb_sc, s_sc, qb_sc, m_sc, l_sc, acc_sc, *, layer, pb, sub):
    b = pl.program_id(0)
    nbatch = pl.num_programs(0)
    npages = pt_ref.shape[1]
    nblk = npages // pb
    tb = pb * PAGE_SIZE
    nsub = tb // sub

    def page_copies(bb, blk, slot):
        copies = []
        for p in range(pb):
            page = pt_ref[bb, blk * pb + p]
            rows = pl.ds(p * PAGE_SIZE, PAGE_SIZE)
            copies.append(pltpu.make_async_copy(ckv_hbm.at[layer, page], cbuf.at[slot, rows], sem.at[slot, 0]))
            copies.append(pltpu.make_async_copy(kr_hbm.at[layer, page], kbuf.at[slot, rows], sem.at[slot, 1]))
        return copies

    @pl.when(b == 0)
    def _():
        for cp in page_copies(0, 0, 0):
            cp.start()

    q = q_ref[0]
    qg = q[:, :QK_NOPE].astype(f32) * gkn_ref[...]
    qg_t = jnp.concatenate([qg, jnp.zeros((LANES - N_HEADS_A, QK_NOPE), f32)], axis=0).T
    for hh in range(N_HEADS_A):
        qb_sc[QK_NOPE * hh:QK_NOPE * (hh + 1), :] = jnp.broadcast_to(qg_t[:, hh:hh + 1], (QK_NOPE, LANES))
    qr16 = jnp.concatenate([q[:, QK_NOPE:QK_NOPE + QK_ROPE],
                            jnp.zeros((16 - N_HEADS_A, QK_ROPE), bf16)], axis=0)

    m_sc[...] = jnp.full(m_sc.shape, -jnp.inf, f32)
    l_sc[...] = jnp.zeros(l_sc.shape, f32)
    acc_sc[...] = jnp.zeros(acc_sc.shape, f32)
    row16 = lax.broadcasted_iota(jnp.int32, (16, sub), 0)

    def block(blk, carry):
        slot = (b * nblk + blk) % 2
        for cp in page_copies(b, blk, slot):
            cp.wait()

        @pl.when(blk + 1 < nblk)
        def _():
            for cp in page_copies(b, blk + 1, 1 - slot):
                cp.start()

        @pl.when(jnp.logical_and(blk + 1 == nblk, b + 1 < nbatch))
        def _():
            for cp in page_copies(b + 1, 0, 1 - slot):
                cp.start()

        for si in range(nsub):
            rs = slice(sub * si, sub * (si + 1))
            cb = cbuf[slot, rs, :].astype(bf16)
            cb_sc[rs, :] = cb
            knt = _dot_nt(wukt_ref[...], cb)
            s16 = _dot_nt(qr16, kbuf[slot, rs, :].astype(bf16))
            for hh in range(N_HEADS_A):
                kh = knt[QK_NOPE * hh:QK_NOPE * (hh + 1), :]
                ss = jnp.sum(kh * kh, axis=0, keepdims=True)
                num = jnp.sum(kh * _rep_lanes(qb_sc[QK_NOPE * hh:QK_NOPE * (hh + 1), :], sub // LANES),
                              axis=0, keepdims=True)
                sn = num * lax.rsqrt(ss * (1.0 / QK_NOPE) + EPS)
                s16 = s16 + jnp.where(row16 == hh, sn, 0.0)
            s_sc[:, rs] = s16

        s = s_sc[...]
        m_old = m_sc[...]
        m_new = jnp.maximum(m_old, jnp.max(s, axis=1, keepdims=True))
        alpha = jnp.exp(m_old - m_new)
        p = jnp.exp(s - _rep_lanes(m_new, tb // LANES))
        l_sc[...] = alpha * l_sc[...] + jnp.sum(p, axis=1, keepdims=True)
        acc_sc[...] = _rep_lanes(alpha, KV_LORA // LANES) * acc_sc[...] + _dot(p.astype(bf16), cb_sc[...])
        m_sc[...] = m_new
        return carry

    lax.fori_loop(0, nblk, block, 0)

    kn = knew_ref[0].astype(f32)
    s_new = jnp.sum(q.astype(f32) * kn, axis=1, keepdims=True)
    s_new = jnp.concatenate([jnp.broadcast_to(s_new, (N_HEADS_A, LANES)),
                             jnp.zeros((16 - N_HEADS_A, LANES), f32)], axis=0)
    m_old = m_sc[...]
    m_new = jnp.maximum(m_old, s_new)
    alpha = jnp.exp(m_old - m_new)
    p_new = jnp.exp(s_new - m_new)
    l = alpha * l_sc[...] + p_new
    acc = _rep_lanes(alpha, 2) * acc_sc[...] + _rep_lanes(p_new, 2) * cnew_ref[0]
    ctx = acc * _rep_lanes(1.0 / l, 2)
    ctx_ref[0] = ctx[0:N_HEADS_A, :]


def _attn_sample(page_table, q_s, k_s, c_s, wp, cache_ckv, cache_krope, layer, pb, sub):
    nbd = q_s.shape[0]
    tb = pb * PAGE_SIZE
    full = lambda a: pl.BlockSpec(a.shape, lambda b, pt: (0,) * a.ndim)
    grid_spec = pltpu.PrefetchScalarGridSpec(
        num_scalar_prefetch=1,
        grid=(nbd,),
        in_specs=[pl.BlockSpec((1, N_HEADS_A, QK_PAD), lambda b, pt: (b, 0, 0)),
                  pl.BlockSpec((1, N_HEADS_A, QK_PAD), lambda b, pt: (b, 0, 0)),
                  pl.BlockSpec((1, 1, KV_LORA), lambda b, pt: (b, 0, 0)),
                  full(wp["w_uk_t"]), full(wp["g_kn"]),
                  pl.BlockSpec(memory_space=pl.ANY),
                  pl.BlockSpec(memory_space=pl.ANY)],
        out_specs=pl.BlockSpec((1, N_HEADS_A, KV_LORA), lambda b, pt: (b, 0, 0)),
        scratch_shapes=[pltpu.VMEM((2, tb, KV_LORA), f32),
                        pltpu.VMEM((2, tb, QK_ROPE), f32),
                        pltpu.SemaphoreType.DMA((2, 2)),
                        pltpu.VMEM((tb, KV_LORA), bf16),
                        pltpu.VMEM((16, tb), f32),
                        pltpu.VMEM((N_HEADS_A * QK_NOPE, LANES), f32),
                        pltpu.VMEM((16, LANES), f32), pltpu.VMEM((16, LANES), f32),
                        pltpu.VMEM((16, KV_LORA), f32)],
    )
    return pl.pallas_call(
        functools.partial(_sattn_kernel, layer=layer, pb=pb, sub=sub),
        grid_spec=grid_spec,
        out_shape=jax.ShapeDtypeStruct((nbd, N_HEADS_A, KV_LORA), f32),
        compiler_params=pltpu.CompilerParams(dimension_semantics=("arbitrary",), vmem_limit_bytes=VMEM_LIMIT),
        name="attn_sample",
    )(page_table, q_s, k_s, c_s, wp["w_uk_t"], wp["g_kn"], cache_ckv, cache_krope)


def _stail_kernel(x_ref, ctx_ref, sza_ref, min_ref, szm_ref, conv_ref, c0_ref, n0_ref, m0_ref,
                  wuv_ref, cw_ref, cb_ref, wqk_ref, wv_ref, wg_ref, gb_ref, lng_ref, skip_ref, woa_ref, wom_ref,
                  y_ref, c1_ref, n1_ref, m1_ref, conv1_ref, *, bb):
    mixa = []
    for hh in range(N_HEADS_A):
        att = _dot(ctx_ref[hh].astype(bf16), wuv_ref[hh])
        sl = slice(V_HEAD * hh, V_HEAD * (hh + 1))
        mixa.append((att * sza_ref[:, sl].astype(f32)).astype(bf16))
    mixa = jnp.concatenate(mixa, axis=1)

    x = min_ref[...]
    y = cb_ref[...] + x * cw_ref[3:4, :]
    for j in range(CONV_W - 1):
        y = y + conv_ref[j] * cw_ref[j:j + 1, :]
    for j in range(CONV_W - 2):
        conv1_ref[j] = conv_ref[j + 1]
    conv1_ref[CONV_W - 2] = x
    ca = _silu(y)
    cab = ca.astype(bf16)
    xb = x.astype(bf16)
    qs, ks, vs = [], [], []
    for hh in range(N_HEADS_M):
        sl = slice(DH_M * hh, DH_M * (hh + 1))
        qk = _dot(cab[:, sl], wqk_ref[hh])
        qs.append(qk[:, :DH_M])
        ks.append(qk[:, DH_M:])
        vs.append(_dot(xb[:, sl], wv_ref[hh]))
    qkv = jnp.concatenate(qs + ks + vs, axis=1).astype(bf16)
    g = _dot(qkv, wg_ref[...]) + gb_ref[...]
    li = g
    lf = jax.nn.log_sigmoid(pltpu.roll(g, LANES - N_HEADS_M, 1))
    m0 = m0_ref[...]
    a = lf + m0
    mt = jnp.maximum(a, li)
    w_inter = jnp.exp(a - mt)
    w_new = jnp.exp(li - mt)
    emt = jnp.exp(-mt)
    m1_ref[...] = mt

    rowi = lax.broadcasted_iota(jnp.int32, (bb, DH_M), 0)
    rowp = lax.broadcasted_iota(jnp.int32, (LANES, DH_M), 0)
    zpad = jnp.zeros((LANES - bb, DH_M), f32)
    mixm = []
    for hh in range(N_HEADS_M):
        sl = slice(DH_M * hh, DH_M * (hh + 1))
        col = lambda z: jnp.broadcast_to(z[:, hh:hh + 1], (bb, DH_M))
        wi, wn, em = col(w_inter), col(w_new), col(emt)
        qh = qs[hh]
        kh = ks[hh] * K_SCALE_M
        vh = vs[hh]
        qb_ = qh.astype(bf16)
        kw = (kh * wn)
        kwt = jnp.concatenate([kw, zpad], axis=0).T.astype(bf16)
        vb_ = vh.astype(bf16)
        vpad = jnp.concatenate([vh, zpad], axis=0)
        inter = jnp.zeros((bb, DH_M), f32)
        for r in range(bb):
            c0 = c0_ref[r, hh]
            inter = inter + jnp.where(rowi == r, _dot(qb_, c0.astype(bf16)), 0.0)
            upd = _dot(kwt, jnp.where(rowp == r, vpad, 0.0).astype(bf16))
            dec = jnp.broadcast_to(wi[r:r + 1, :], (DH_M, DH_M))
            c1_ref[r, hh] = dec * c0 + upd
        n0 = n0_ref[hh]
        n1_ref[hh] = wi * n0 + kw
        qk_dot = jnp.sum(qb_.astype(f32) * kh.astype(bf16).astype(f32), axis=1, keepdims=True) * wn
        num = wi * inter + qk_dot * vb_.astype(f32)
        den = wi * jnp.sum(qb_.astype(f32) * n0.astype(bf16).astype(f32), axis=1, keepdims=True) + qk_dot
        hv = num / jnp.maximum(jnp.abs(den), em)
        mu = jnp.mean(hv, axis=-1, keepdims=True)
        hc = hv - mu
        var = jnp.mean(hc * hc, axis=-1, keepdims=True)
        hn = hc * lax.rsqrt(var + LN_EPS) * lng_ref[:, sl]
        out = hn + skip_ref[:, sl] * ca[:, sl]
        mixm.append((out * szm_ref[:, sl].astype(f32)).astype(bf16))
    mixm = jnp.concatenate(mixm, axis=1)
    y_ref[...] = x_ref[...] + _dot(mixa, woa_ref[...]) + _dot(mixm, wom_ref[...])


def _sample_tail(x_s, ctx, sza, m_in, szm, conv0, c0, n0, m0, wp, bb):
    nbd = x_s.shape[0]
    full = lambda a: pl.BlockSpec(a.shape, lambda i: (0,) * a.ndim)
    row = lambda w: pl.BlockSpec((bb, w), lambda i: (i, 0))
    weights = (wp["w_uv"], wp["conv_w"], wp["conv_b"], wp["w_qk_m"], wp["w_v_m"], wp["w_gate"], wp["gate_b"],
               wp["mh_norm_g"], wp["skip"], wp["w_out_a"], wp["w_out_m"])
    conv_spec = pl.BlockSpec((CONV_W - 1, bb, D_MLS), lambda i: (0, i, 0))
    c_spec = pl.BlockSpec((bb, N_HEADS_M, DH_M, DH_M), lambda i: (i, 0, 0, 0))
    n_spec = pl.BlockSpec((N_HEADS_M, bb, DH_M), lambda i: (0, i, 0))
    return pl.pallas_call(
        functools.partial(_stail_kernel, bb=bb),
        grid=(nbd // bb,),
        in_specs=[row(D_MODEL),
                  pl.BlockSpec((N_HEADS_A, bb, KV_LORA), lambda i: (0, i, 0)),
                  row(D_ATT), row(D_MLS), row(D_MLS), conv_spec, c_spec, n_spec, row(LANES)]
                 + [full(w) for w in weights],
        out_specs=(row(D_MODEL), c_spec, n_spec, row(LANES), conv_spec),
        out_shape=(jax.ShapeDtypeStruct((nbd, D_MODEL), f32),
                   jax.ShapeDtypeStruct(c0.shape, f32),
                   jax.ShapeDtypeStruct(n0.shape, f32),
                   jax.ShapeDtypeStruct((nbd, LANES), f32),
                   jax.ShapeDtypeStruct(conv0.shape, f32)),
        compiler_params=pltpu.CompilerParams(dimension_semantics=("arbitrary",), vmem_limit_bytes=VMEM_LIMIT),
        name="sample_tail",
    )(x_s, ctx, sza, m_in, szm, conv0, c0, n0, m0, *weights)


def _prep_weights(l, norm_g, w_in, q_norm_g, w_uq, kv_norm_g, w_uk, w_uv, g_qn, g_qr, g_kn, g_kr,
                  conv_w, conv_b, w_q_m, w_k_m, w_v_m, w_gate, b_i, b_f, mh_norm_g, skip, w_out):
    wi = w_in[l]
    o1 = Q_LORA + KV_LORA + QK_ROPE
    w1 = jnp.concatenate([wi[:, :o1], jnp.zeros((D_MODEL, LANES - QK_ROPE), f32)], axis=1)
    w2 = wi[:, o1:]
    wq = w_uq[l].reshape(Q_LORA, N_HEADS_A, QK_NOPE + QK_ROPE)
    wq_rope = jnp.concatenate([wq[:, :, QK_NOPE:], jnp.zeros((Q_LORA, N_HEADS_A, LANES - QK_ROPE), f32)], axis=2)
    wuq = jnp.concatenate([wq[:, :, :QK_NOPE].reshape(Q_LORA, -1), wq_rope.reshape(Q_LORA, -1)], axis=1)
    pad_rope = lambda g: jnp.concatenate([g, jnp.zeros((LANES - QK_ROPE,), f32)])[None, :]
    wuk = w_uk[l].reshape(KV_LORA, N_HEADS_A * QK_NOPE)
    wg = jnp.concatenate([w_gate[l], jnp.zeros((3 * D_MLS, LANES - 2 * N_HEADS_M), f32)], axis=1)
    gate_b = jnp.concatenate([b_i[l], b_f[l], jnp.zeros((LANES - 2 * N_HEADS_M,), f32)])[None, :]
    return {
        "norm_g": norm_g[l][None, :],
        "w1": w1.astype(bf16),
        "w2": w2.astype(bf16),
        "q_norm_g": q_norm_g[l][None, :],
        "w_uq": wuq.astype(bf16),
        "kv_norm_g": kv_norm_g[l][None, :],
        "w_uk": wuk.astype(bf16),
        "w_uk_t": wuk.T.astype(bf16),
        "w_uv": jnp.transpose(w_uv[l], (1, 0, 2)).astype(bf16),
        "g_qn": g_qn[l][None, :],
        "g_qr": pad_rope(g_qr[l]),
        "g_kn": g_kn[l][None, :],
        "g_kr": pad_rope(g_kr[l]),
        "conv_w": conv_w[l],
        "conv_b": conv_b[l][None, :],
        "w_qk_m": jnp.concatenate([w_q_m[l], w_k_m[l]], axis=2).astype(bf16),
        "w_v_m": w_v_m[l].astype(bf16),
        "w_gate": wg.astype(bf16),
        "gate_b": gate_b,
        "mh_norm_g": mh_norm_g[l][None, :],
        "skip": skip[l][None, :],
        "w_out_a": w_out[l][:D_ATT].astype(bf16),
        "w_out_m": w_out[l][D_ATT:].astype(bf16),
    }


def _pick(n, candidates):
    for c in candidates:
        if n % c == 0:
            return c
    raise ValueError(f"no tile size for extent {n}")


def kernel(x_prompt, x_sample, cache_ckv, cache_krope, state_C, state_n, state_m, state_conv, page_table,
           norm_g, w_in, q_norm_g, w_uq, kv_norm_g, w_uk, w_uv, g_qn, g_qr, g_kn, g_kr,
           conv_w, conv_b, w_q_m, w_k_m, w_v_m, w_gate, b_i, b_f, mh_norm_g, skip, w_out):
    nb, s, _ = x_prompt.shape
    nbd, sd, _ = x_sample.shape
    depth = norm_g.shape[0]
    assert sd == 1, "sample path handles one new token per sequence"
    npages = page_table.shape[1]
    past_len = npages * PAGE_SIZE

    cos_p, sin_p = _rope_tables(jnp.arange(s))
    cos_s, sin_s = _rope_tables(jnp.full((nbd,), past_len, jnp.int32))

    tm_p = _pick(s, (512, 256, 128))
    t_att = _pick(s, (256, 128))
    ts_m = _pick(s, (512, 256, 128))
    tm_o = _pick(nb * s, (512, 256, 128))
    pb = _pick(npages, (16, 8, 4, 2, 1))
    sub = _pick(pb * PAGE_SIZE, (512, 256, 128))
    bb = _pick(nbd, (16,))

    yp = x_prompt
    ys = x_sample.reshape(1, nbd, D_MODEL)
    outs_p, outs_s = [], []
    for l in range(depth):
        wp = _prep_weights(l, norm_g, w_in, q_norm_g, w_uq, kv_norm_g, w_uk, w_uv, g_qn, g_qr, g_kn, g_kr,
                           conv_w, conv_b, w_q_m, w_k_m, w_v_m, w_gate, b_i, b_f, mh_norm_g, skip, w_out)
        q, k, ckv, ckvb, kr, sza, m_in, szm = _proj(yp, cos_p, sin_p, wp, tm_p)
        mixa = _attn_prompt(q, k, ckvb, sza, wp["w_uv"], t_att)
        mixm, cn, mm, halo = _mlstm_prompt(m_in, szm, wp, ts_m, LANES)
        yp = _out_proj(yp.reshape(nb * s, D_MODEL), mixa.reshape(nb * s, D_ATT), mixm.reshape(nb * s, D_MLS),
                       wp, tm_o).reshape(nb, s, D_MODEL)
        outs_p.append((ckv, kr, cn[..., :DH_M], cn[..., DH_M], mm[:, :N_HEADS_M, 0],
                       halo[:, 8 - (CONV_W - 1):, :]))
        q_s, k_s, ckv_s, _, kr_s, sza_s, m_in_s, szm_s = _proj(ys, cos_s, sin_s, wp, nbd)
        ctx = _attn_sample(page_table, jnp.transpose(q_s[0], (1, 0, 2)), jnp.transpose(k_s[0], (1, 0, 2)),
                           ckv_s.reshape(nbd, 1, KV_LORA), wp, cache_ckv, cache_krope, l, pb, sub)
        m0 = jnp.concatenate([state_m[l], jnp.zeros((nbd, LANES - N_HEADS_M), f32)], axis=1)
        to_lead = lambda a: jnp.transpose(a, (1, 0, 2))
        y_s, c1, n1, m1, conv1 = _sample_tail(ys[0], to_lead(ctx), sza_s[0], m_in_s[0], szm_s[0],
                                              to_lead(state_conv[l]), state_C[l], to_lead(state_n[l]), m0, wp, bb)
        ys = y_s.reshape(1, nbd, D_MODEL)
        outs_s.append((ckv_s.reshape(nbd, 1, KV_LORA), kr_s.reshape(nbd, 1, QK_ROPE), c1, to_lead(n1),
                       m1[:, :N_HEADS_M], to_lead(conv1)))
    stk = lambda outs, i: jnp.stack([o[i] for o in outs], axis=0)
    return (yp, ys.reshape(nbd, 1, D_MODEL),
            stk(outs_p, 0), stk(outs_p, 1), stk(outs_p, 2), stk(outs_p, 3), stk(outs_p, 4), stk(outs_p, 5),
            stk(outs_s, 0), stk(outs_s, 1), stk(outs_s, 2), stk(outs_s, 3), stk(outs_s, 4), stk(outs_s, 5))
```

```python
import functools

import jax
import jax.numpy as jnp
from jax import lax
from jax.experimental import pallas as pl
from jax.experimental.pallas import tpu as pltpu

f32 = jnp.float32
bf16 = jnp.bfloat16

D_MODEL = 1024
D_ATT = 512
D_MLS = 512
N_HEADS_A = 4
QK_NOPE = 128
QK_ROPE = 64
V_HEAD = 128
Q_LORA = 384
KV_LORA = 256
ROPE_BASE = 10000.0
N_HEADS_M = 4
DH_M = 128
CONV_W = 4
PAGE_SIZE = 128
EPS = 1e-6
LN_EPS = 1e-5
ATT_SCALE = (QK_NOPE + QK_ROPE) ** -0.5
LOG2E = 1.4426950408889634
Q_SCALE = ATT_SCALE * LOG2E
K_SCALE_M = DH_M ** -0.5
PROJ_ROWS = 256

LANES = 128
QK_PAD = 256
VMEM_LIMIT = 56 * 1024 * 1024

NT_DIMS = (((1,), (1,)), ((), ()))


def _rms(x, n):
    ms = jnp.sum(x * x, axis=-1, keepdims=True) * (1.0 / n)
    return x * lax.rsqrt(ms + EPS)


def _silu(x):
    return x * jax.nn.sigmoid(x)


def _dot(a, b):
    return jnp.dot(a, b, preferred_element_type=f32)


def _dot_nt(a, b):
    return lax.dot_general(a, b, NT_DIMS, preferred_element_type=f32)


def _rep_lanes(x, n):
    return x if n == 1 else jnp.concatenate([x] * n, axis=1)


def _rope_table_kernel(ang_ref, cos_ref, sin_ref):
    ang = ang_ref[...]
    lane = lax.broadcasted_iota(jnp.int32, ang.shape, 1)
    cos_ref[...] = jnp.cos(ang)
    s = jnp.sin(ang)
    sin_ref[...] = jnp.where((lane % QK_ROPE) < QK_ROPE // 2, -s, s)


def _rope_tables(pos):
    half = QK_ROPE // 2
    inv = 1.0 / (ROPE_BASE ** (jnp.arange(0, QK_ROPE, 2, dtype=f32) / QK_ROPE))
    ang = pos.astype(f32)[:, None] * inv[None, :]
    ang = jnp.tile(ang, (1, LANES // half))
    s = ang.shape[0]
    return pl.pallas_call(
        _rope_table_kernel,
        out_shape=(jax.ShapeDtypeStruct((s, LANES), f32), jax.ShapeDtypeStruct((s, LANES), f32)),
        name="rope_tables",
    )(ang)


def _proj_kernel(x_ref, cos_ref, sin_ref, ng_ref, w1_ref, w2_ref, qg_ref, wuq_ref, kvg_ref, wuk_ref,
                 gqn_ref, gqr_ref, gkn_ref, gkr_ref,
                 q_ref, k_ref, ckv_ref, ckvb_ref, kr_ref, sza_ref, min_ref, szm_ref):
    tm = x_ref.shape[1]
    rows = min(tm, PROJ_ROWS)
    lane = lax.broadcasted_iota(jnp.int32, (rows, LANES), 1)
    first_half = (lane % QK_ROPE) < QK_ROPE // 2

    for r0 in range(0, tm, rows):
        rs = slice(r0, r0 + rows)
        x = x_ref[0, rs, :]
        h = (_rms(x, D_MODEL) * ng_ref[...]).astype(bf16)
        p1 = _dot(h, w1_ref[...])
        p2 = _dot(h, w2_ref[...])
        cos = cos_ref[rs, :]
        sin = sin_ref[rs, :]

        def rope(xp):
            sw = jnp.where(first_half, pltpu.roll(xp, LANES - QK_ROPE // 2, 1), pltpu.roll(xp, QK_ROPE // 2, 1))
            return xp * cos + sw * sin

        ql = (_rms(p1[:, :Q_LORA], Q_LORA) * qg_ref[...]).astype(bf16)
        qf = _dot(ql, wuq_ref[...])
        for hh in range(N_HEADS_A):
            qn = _rms(qf[:, LANES * hh:LANES * (hh + 1)], QK_NOPE) * gqn_ref[...]
            q_ref[0, hh, rs, 0:LANES] = (qn * Q_SCALE).astype(bf16)
            o = N_HEADS_A * QK_NOPE + LANES * hh
            qr = rope(_rms(qf[:, o:o + LANES], QK_ROPE) * gqr_ref[...])
            q_ref[0, hh, rs, LANES:QK_PAD] = (qr * Q_SCALE).astype(bf16)

        c = _rms(p1[:, Q_LORA:Q_LORA + KV_LORA], KV_LORA) * kvg_ref[...]
        ckv_ref[0, rs, :] = c
        cb = c.astype(bf16)
        ckvb_ref[0, rs, :] = cb
        kn = _dot(cb, wuk_ref[...])
        o = Q_LORA + KV_LORA
        krp = rope(_rms(p1[:, o:o + LANES], QK_ROPE) * gkr_ref[...])
        kr_ref[0, rs, :] = krp[:, :QK_ROPE]
        krb = krp.astype(bf16)
        for hh in range(N_HEADS_A):
            knh = _rms(kn[:, LANES * hh:LANES * (hh + 1)], QK_NOPE) * gkn_ref[...]
            k_ref[0, hh, rs, 0:LANES] = knh.astype(bf16)
            k_ref[0, hh, rs, LANES:QK_PAD] = krb

        sza_ref[0, rs, :] = _silu(p2[:, :D_ATT]).astype(bf16)
        min_ref[0, rs, :] = p2[:, D_ATT:D_ATT + D_MLS]
        szm_ref[0, rs, :] = _silu(p2[:, D_ATT + D_MLS:]).astype(bf16)


def _proj(x, cos, sin, wp, tm):
    nb, s, _ = x.shape
    grid = (nb, s // tm)
    full = lambda a: pl.BlockSpec(a.shape, lambda b, i: (0,) * a.ndim)
    tok = lambda w: pl.BlockSpec((1, tm, w), lambda b, i: (b, i, 0))
    head = pl.BlockSpec((1, N_HEADS_A, tm, QK_PAD), lambda b, i: (b, 0, i, 0))
    tab = pl.BlockSpec((tm, LANES), lambda b, i: (i, 0))
    weights = (wp["norm_g"], wp["w1"], wp["w2"], wp["q_norm_g"], wp["w_uq"], wp["kv_norm_g"], wp["w_uk"],
               wp["g_qn"], wp["g_qr"], wp["g_kn"], wp["g_kr"])
    out_shape = (
        jax.ShapeDtypeStruct((nb, N_HEADS_A, s, QK_PAD), bf16),
        jax.ShapeDtypeStruct((nb, N_HEADS_A, s, QK_PAD), bf16),
        jax.ShapeDtypeStruct((nb, s, KV_LORA), f32),
        jax.ShapeDtypeStruct((nb, s, KV_LORA), bf16),
        jax.ShapeDtypeStruct((nb, s, QK_ROPE), f32),
        jax.ShapeDtypeStruct((nb, s, D_ATT), bf16),
        jax.ShapeDtypeStruct((nb, s, D_MLS), f32),
        jax.ShapeDtypeStruct((nb, s, D_MLS), bf16),
    )
    out_specs = (head, head, tok(KV_LORA), tok(KV_LORA), tok(QK_ROPE), tok(D_ATT), tok(D_MLS), tok(D_MLS))
    return pl.pallas_call(
        _proj_kernel,
        grid=grid,
        in_specs=[tok(D_MODEL), tab, tab] + [full(w) for w in weights],
        out_specs=out_specs,
        out_shape=out_shape,
        compiler_params=pltpu.CompilerParams(dimension_semantics=("arbitrary", "arbitrary"),
                                             vmem_limit_bytes=VMEM_LIMIT),
        name="in_proj",
    )(x, cos, sin, *weights)


def _attn_kernel(q_ref, k_ref, c_ref, sza_ref, wuv_ref, o_ref, m_sc, l_sc, acc_sc, s_sc, *, t):
    i = pl.program_id(1)
    nrep = t // LANES
    row = lax.broadcasted_iota(jnp.int32, (t, t), 0)
    col = lax.broadcasted_iota(jnp.int32, (t, t), 1)
    causal = col <= row

    m_sc[...] = jnp.full(m_sc.shape, -jnp.inf, f32)
    l_sc[...] = jnp.zeros(l_sc.shape, f32)
    acc_sc[...] = jnp.zeros(acc_sc.shape, f32)

    def step(j, masked):
        start = pl.multiple_of(j * t, t)
        cj = c_ref[0, pl.ds(start, t), :]

        def scores(hh):
            s = _dot_nt(q_ref[0, hh], k_ref[0, hh, pl.ds(start, t), :])
            s_sc[hh] = jnp.where(causal, s, -jnp.inf) if masked else s

        def update(hh):
            s = s_sc[hh]
            m_old = m_sc[hh]
            m_new = jnp.maximum(m_old, jnp.max(s, axis=1, keepdims=True))
            alpha = jnp.exp2(m_old - m_new)
            p = jnp.exp2(s - _rep_lanes(m_new, nrep))
            l_sc[hh] = alpha * l_sc[hh] + jnp.sum(p, axis=1, keepdims=True)
            acc_sc[hh] = _rep_lanes(alpha, KV_LORA // LANES) * acc_sc[hh] + _dot(p.astype(bf16), cj)
            m_sc[hh] = m_new

        scores(0)
        for hh in range(N_HEADS_A):
            if hh + 1 < N_HEADS_A:
                scores(hh + 1)
            update(hh)

    def body(j, carry):
        step(j, False)
        return carry

    lax.fori_loop(0, i, body, 0)
    step(i, True)

    for hh in range(N_HEADS_A):
        inv = 1.0 / l_sc[hh]
        ctx = (acc_sc[hh] * _rep_lanes(inv, KV_LORA // LANES)).astype(bf16)
        att = _dot(ctx, wuv_ref[hh])
        sl = slice(V_HEAD * hh, V_HEAD * (hh + 1))
        o_ref[0, :, sl] = (att * sza_ref[0, :, sl].astype(f32)).astype(bf16)


def _attn_prompt(q, k, cb, sza, wuv, t):
    nb, _, s, _ = q.shape
    return pl.pallas_call(
        functools.partial(_attn_kernel, t=t),
        grid=(nb, s // t),
        in_specs=[
            pl.BlockSpec((1, N_HEADS_A, t, QK_PAD), lambda b, i: (b, 0, i, 0)),
            pl.BlockSpec((1, N_HEADS_A, s, QK_PAD), lambda b, i: (b, 0, 0, 0)),
            pl.BlockSpec((1, s, KV_LORA), lambda b, i: (b, 0, 0)),
            pl.BlockSpec((1, t, D_ATT), lambda b, i: (b, i, 0)),
            pl.BlockSpec(wuv.shape, lambda b, i: (0, 0, 0)),
        ],
        out_specs=pl.BlockSpec((1, t, D_ATT), lambda b, i: (b, i, 0)),
        out_shape=jax.ShapeDtypeStruct((nb, s, D_ATT), bf16),
        scratch_shapes=[pltpu.VMEM((N_HEADS_A, t, LANES), f32),
                        pltpu.VMEM((N_HEADS_A, t, LANES), f32),
                        pltpu.VMEM((N_HEADS_A, t, KV_LORA), f32),
                        pltpu.VMEM((N_HEADS_A, t, t), f32)],
        compiler_params=pltpu.CompilerParams(dimension_semantics=("arbitrary", "arbitrary"),
                                             vmem_limit_bytes=VMEM_LIMIT),
        name="attn_prompt",
    )(q, k, cb, sza, wuv)


def _scan_lanes(x, op, fill, length):
    lane = lax.broadcasted_iota(jnp.int32, x.shape, 1)
    d = 1
    while d < length:
        x = op(x, jnp.where(lane >= d, pltpu.roll(x, d, 1), fill))
        d *= 2
    return x


def _mlstm_kernel(min_ref, szm_ref, cw_ref, cb_ref, wqk_ref, wv_ref, wg_ref, gb_ref, lng_ref, skip_ref, sel_ref,
                  mix_ref, cn_ref, m_ref, halo_ref,
                  xbuf, ca_sc, qkv_sc, ks_sc, cn_sc, m_sc, *, ts, lc):
    si = pl.program_id(1)
    ns = pl.num_programs(1)
    nc = ts // lc

    @pl.when(si == 0)
    def _():
        xbuf[0:8, :] = jnp.zeros((8, D_MLS), f32)
        cn_sc[...] = jnp.zeros(cn_sc.shape, f32)
        m_sc[...] = jnp.zeros(m_sc.shape, f32)

    x = min_ref[0]
    xbuf[8:8 + ts, :] = x
    y = cb_ref[...] + xbuf[8:8 + ts, :] * cw_ref[3:4, :]
    for j in range(CONV_W - 1):
        y = y + xbuf[5 + j:5 + j + ts, :] * cw_ref[j:j + 1, :]
    xbuf[0:8, :] = xbuf[ts:ts + 8, :]
    ca = _silu(y)
    ca_sc[...] = ca
    cab = ca.astype(bf16)
    xb = x.astype(bf16)
    for hh in range(N_HEADS_M):
        sl = slice(DH_M * hh, DH_M * (hh + 1))
        qk = _dot(cab[:, sl], wqk_ref[hh])
        qkv_sc[:, sl] = qk[:, :DH_M].astype(bf16)
        kh = qk[:, DH_M:]
        qkv_sc[:, D_MLS + DH_M * hh:D_MLS + DH_M * (hh + 1)] = kh.astype(bf16)
        ks_sc[:, sl] = (kh * K_SCALE_M).astype(bf16)
        qkv_sc[:, 2 * D_MLS + DH_M * hh:2 * D_MLS + DH_M * (hh + 1)] = _dot(xb[:, sl], wv_ref[hh]).astype(bf16)

    g = _dot(qkv_sc[...], wg_ref[...]) + gb_ref[...]
    gt = g.T[0:8, :]
    head_row = lax.broadcasted_iota(jnp.int32, gt.shape, 0) < N_HEADS_M
    li_all = jnp.where(head_row, gt, 0.0)
    lf_all = jnp.where(head_row, jax.nn.log_sigmoid(pltpu.roll(gt, 4, 0)), 0.0)

    row = lax.broadcasted_iota(jnp.int32, (lc, lc), 0)
    col = lax.broadcasted_iota(jnp.int32, (lc, lc), 1)
    causal = col <= row
    ones_b = jnp.ones((lc, LANES), bf16)

    for c in range(nc):
        cs = slice(lc * c, lc * (c + 1))
        li = li_all[:, cs]
        lf = lf_all[:, cs]
        m0 = m_sc[...]
        b = _scan_lanes(lf, jnp.add, 0.0, lc)
        u = li - b
        gmax = jnp.maximum(_rep_lanes(m0, lc // LANES), _scan_lanes(u, jnp.maximum, -jnp.inf, lc))
        mt = b + gmax
        g_last = jnp.maximum(m0, jnp.max(u, axis=1, keepdims=True))
        m_new = jnp.sum(lf, axis=1, keepdims=True) + g_last
        decay = jnp.exp(m0 - g_last)
        w_s = jnp.exp(u - _rep_lanes(g_last, lc // LANES))
        r = jnp.concatenate([gmax, mt], axis=0)
        r_hi = r.astype(bf16).astype(f32)
        r_mid = (r - r_hi).astype(bf16).astype(f32)
        r_lo = r - r_hi - r_mid
        stack = jnp.concatenate([r_hi, r_mid, r_lo, jnp.zeros((lc - 48, lc), f32)], axis=0)
        cols = _dot(stack.T.astype(bf16), sel_ref[...])

        for hh in range(N_HEADS_M):
            sl = slice(DH_M * hh, DH_M * (hh + 1))
            g_col = cols[:, LANES * hh:LANES * (hh + 1)]
            mt_col = cols[:, LANES * (N_HEADS_M + hh):LANES * (N_HEADS_M + hh + 1)]
            qc = qkv_sc[cs, sl]
            kc = ks_sc[cs, sl]
            vc = qkv_sc[cs, 2 * D_MLS + DH_M * hh:2 * D_MLS + DH_M * (hh + 1)]
            s_qk = _dot_nt(qc, kc)
            wm = jnp.where(causal, jnp.exp(u[hh:hh + 1, :] - g_col), 0.0)
            p = (s_qk * wm).astype(bf16)
            v_aug = jnp.concatenate([vc, ones_b], axis=1)
            intra = _dot(p, v_aug)
            inter = _dot(qc, cn_sc[hh].astype(bf16))
            w_inter = jnp.exp(m0[hh:hh + 1, :] - g_col)
            tot = _rep_lanes(w_inter, 2) * inter + intra
            num = tot[:, :DH_M]
            den = tot[:, DH_M:]
            hv = num / jnp.maximum(jnp.abs(den), jnp.exp(-mt_col))
            mu = jnp.mean(hv, axis=-1, keepdims=True)
            hc = hv - mu
            var = jnp.mean(hc * hc, axis=-1, keepdims=True)
            hn = hc * lax.rsqrt(var + LN_EPS) * lng_ref[:, sl]
            out = hn + skip_ref[:, sl] * ca_sc[cs, sl]
            mix_ref[0, cs, sl] = (out * szm_ref[0, cs, sl].astype(f32)).astype(bf16)
            ktw = (kc.astype(f32).T * w_s[hh:hh + 1, :]).astype(bf16)
            upd = _dot(ktw, v_aug)
            dec = jnp.broadcast_to(decay[hh:hh + 1, :], (DH_M, LANES))
            cn_sc[hh] = _rep_lanes(dec, 2) * cn_sc[hh] + upd
        m_sc[...] = m_new

    @pl.when(si == ns - 1)
    def _():
        cn_ref[0] = cn_sc[...]
        m_ref[0] = m_sc[...]
        halo_ref[0] = xbuf[0:8, :]


def _column_selector():
    rows = lax.broadcasted_iota(jnp.int32, (LANES, 2 * N_HEADS_M * LANES), 0)
    block = lax.broadcasted_iota(jnp.int32, (LANES, 2 * N_HEADS_M * LANES), 1) // LANES
    src = 8 * (block // N_HEADS_M) + block % N_HEADS_M
    return ((rows < 48) & (rows % 16 == src)).astype(bf16)


def _mlstm_prompt(m_in, szm, wp, ts, lc):
    nb, s, _ = m_in.shape
    full = lambda a: pl.BlockSpec(a.shape, lambda b, i: (0,) * a.ndim)
    tok = pl.BlockSpec((1, ts, D_MLS), lambda b, i: (b, i, 0))
    assert lc == LANES, "one chunk spans exactly one vreg of lanes"
    weights = (wp["conv_w"], wp["conv_b"], wp["w_qk_m"], wp["w_v_m"], wp["w_gate"], wp["gate_b"],
               wp["mh_norm_g"], wp["skip"], _column_selector())
    return pl.pallas_call(
        functools.partial(_mlstm_kernel, ts=ts, lc=lc),
        grid=(nb, s // ts),
        in_specs=[tok, tok] + [full(w) for w in weights],
        out_specs=(tok,
                   pl.BlockSpec((1, N_HEADS_M, DH_M, 2 * DH_M), lambda b, i: (b, 0, 0, 0)),
                   pl.BlockSpec((1, 8, LANES), lambda b, i: (b, 0, 0)),
                   pl.BlockSpec((1, 8, D_MLS), lambda b, i: (b, 0, 0))),
        out_shape=(jax.ShapeDtypeStruct((nb, s, D_MLS), bf16),
                   jax.ShapeDtypeStruct((nb, N_HEADS_M, DH_M, 2 * DH_M), f32),
                   jax.ShapeDtypeStruct((nb, 8, LANES), f32),
                   jax.ShapeDtypeStruct((nb, 8, D_MLS), f32)),
        scratch_shapes=[pltpu.VMEM((ts + 8, D_MLS), f32),
                        pltpu.VMEM((ts, D_MLS), f32),
                        pltpu.VMEM((ts, 3 * D_MLS), bf16),
                        pltpu.VMEM((ts, D_MLS), bf16),
                        pltpu.VMEM((N_HEADS_M, DH_M, 2 * DH_M), f32),
                        pltpu.VMEM((8, LANES), f32)],
        compiler_params=pltpu.CompilerParams(dimension_semantics=("arbitrary", "arbitrary"),
                                             vmem_limit_bytes=VMEM_LIMIT),
        name="mlstm_prompt",
    )(m_in, szm, *weights)


def _out_kernel(x_ref, ma_ref, mm_ref, wa_ref, wm_ref, y_ref):
    y_ref[...] = x_ref[...] + _dot(ma_ref[...], wa_ref[...]) + _dot(mm_ref[...], wm_ref[...])


def _out_proj(x2, mixa, mixm, wp, tm):
    n = x2.shape[0]
    full = lambda a: pl.BlockSpec(a.shape, lambda i: (0,) * a.ndim)
    return pl.pallas_call(
        _out_kernel,
        grid=(n // tm,),
        in_specs=[pl.BlockSpec((tm, D_MODEL), lambda i: (i, 0)),
                  pl.BlockSpec((tm, D_ATT), lambda i: (i, 0)),
                  pl.BlockSpec((tm, D_MLS), lambda i: (i, 0)),
                  full(wp["w_out_a"]), full(wp["w_out_m"])],
        out_specs=pl.BlockSpec((tm, D_MODEL), lambda i: (i, 0)),
        out_shape=jax.ShapeDtypeStruct((n, D_MODEL), f32),
        compiler_params=pltpu.CompilerParams(dimension_semantics=("arbitrary",), vmem_limit_bytes=VMEM_LIMIT),
        name="out_proj",
    )(x2, mixa, mixm, wp["w_out_a"], wp["w_out_m"])


def _sattn_kernel(pt_ref, q_ref, knew_ref, cnew_ref, wukt_ref, gkn_ref, ckv_hbm, krt_hbm, ctx_ref,
                  cbuf0, cbuf1, kbuf0, kbuf1, sem, cb_sc, knt_sc, *, layer, pb, sub):
    b = pl.program_id(0)
    nbatch = pl.num_programs(0)
    npages = pt_ref.shape[1]
    nblk = npages // pb
    tb = pb * PAGE_SIZE
    nsub = tb // sub
    sub_pages = sub // PAGE_SIZE
    cbufs = (cbuf0, cbuf1)
    kbufs = (kbuf0, kbuf1)

    def page_copies(bb, blk, slot):
        latent, ropek = [], []
        for p in range(pb):
            page = pt_ref[bb, blk * pb + p]
            rows = pl.ds(p * PAGE_SIZE, PAGE_SIZE)
            latent.append(pltpu.make_async_copy(ckv_hbm.at[layer, page], cbufs[slot].at[rows], sem.at[slot, 0]))
            ropek.append(pltpu.make_async_copy(krt_hbm.at[layer, page], kbufs[slot].at[p], sem.at[slot, 1]))
        return latent + ropek

    def start_block(bb, blk, slot):
        for cp in page_copies(bb, blk, slot):
            cp.start()

    def wait_block(bb, blk, slot):
        for cp in page_copies(bb, blk, slot):
            cp.wait()

    @pl.when(b == 0)
    def _():
        start_block(0, 0, 0)

    q = q_ref[0]
    qg = q[:, :QK_NOPE].astype(f32) * gkn_ref[...]
    rowq = lax.broadcasted_iota(jnp.int32, (16, QK_NOPE), 0)
    qabs = jnp.zeros((16, KV_LORA), f32)
    for hh in range(N_HEADS_A):
        lhs = jnp.where(rowq == hh, jnp.broadcast_to(qg[hh:hh + 1, :], (16, QK_NOPE)), 0.0).astype(bf16)
        qabs = qabs + _dot(lhs, wukt_ref[QK_NOPE * hh:QK_NOPE * (hh + 1), :])
    qabs = qabs.astype(bf16)
    qr16 = jnp.concatenate([q[:, QK_NOPE:QK_NOPE + QK_ROPE],
                            jnp.zeros((16 - N_HEADS_A, QK_ROPE), bf16)], axis=0)

    row16 = lax.broadcasted_iota(jnp.int32, (16, sub), 0)

    def matmuls(slot, si):
        cb = cbufs[slot][sub * si:sub * (si + 1), :].astype(bf16)
        cb_sc[si % 2] = cb
        knt_sc[si % 2] = _dot_nt(wukt_ref[...], cb)
        num = _dot_nt(qabs, cb)
        srope = jnp.concatenate(
            [_dot(qr16, kbufs[slot][sub_pages * si + p].astype(bf16)) for p in range(sub_pages)],
            axis=1)
        return num, srope

    def softmax_update(si, staged, carry):
        num, srope = staged
        m, l, acc = carry
        rnorm = jnp.zeros((16, sub), f32)
        for hh in range(N_HEADS_A):
            kh = knt_sc[si % 2, QK_NOPE * hh:QK_NOPE * (hh + 1), :]
            ss = jnp.sum(kh * kh, axis=0, keepdims=True)
            rnorm = jnp.where(row16 == hh, lax.rsqrt(ss * (1.0 / QK_NOPE) + EPS), rnorm)
        s = num * rnorm + srope
        m_new = jnp.maximum(m, jnp.max(s, axis=1, keepdims=True))
        alpha = jnp.exp2(m - m_new)
        p = jnp.exp2(s - _rep_lanes(m_new, sub // LANES))
        l = alpha * l + jnp.sum(p, axis=1, keepdims=True)
        acc = _rep_lanes(alpha, KV_LORA // LANES) * acc + _dot(p.astype(bf16), cb_sc[si % 2])
        return m_new, l, acc

    def compute(slot, carry):
        staged = matmuls(slot, 0)
        for si in range(nsub):
            nxt = matmuls(slot, si + 1) if si + 1 < nsub else None
            carry = softmax_update(si, staged, carry)
            staged = nxt
        return carry

    def pair(i, carry):
        blk = 2 * i
        wait_block(b, blk, 0)
        start_block(b, blk + 1, 1)
        carry = compute(0, carry)
        wait_block(b, blk + 1, 1)
        wraps = blk + 2 == nblk
        start_block(jnp.where(wraps, jnp.minimum(b + 1, nbatch - 1), b), jnp.where(wraps, 0, blk + 2), 0)
        return compute(1, carry)

    init = (jnp.full((16, LANES), -jnp.inf, f32), jnp.zeros((16, LANES), f32), jnp.zeros((16, KV_LORA), f32))
    m_old, l_old, acc_old = lax.fori_loop(0, nblk // 2, pair, init)

    @pl.when(b == nbatch - 1)
    def _():
        wait_block(b, 0, 0)

    kn = knew_ref[0].astype(f32)
    s_new = jnp.sum(q.astype(f32) * kn, axis=1, keepdims=True)
    s_new = jnp.concatenate([jnp.broadcast_to(s_new, (N_HEADS_A, LANES)),
                             jnp.zeros((16 - N_HEADS_A, LANES), f32)], axis=0)
    m_new = jnp.maximum(m_old, s_new)
    alpha = jnp.exp2(m_old - m_new)
    p_new = jnp.exp2(s_new - m_new)
    l = alpha * l_old + p_new
    acc = _rep_lanes(alpha, 2) * acc_old + _rep_lanes(p_new, 2) * cnew_ref[0]
    ctx = acc * _rep_lanes(1.0 / l, 2)
    ctx_ref[0] = ctx[0:N_HEADS_A, :]


def _attn_sample(page_table, q_s, k_s, c_s, wp, cache_ckv, cache_krope_t, layer, pb, sub):
    nbd = q_s.shape[0]
    tb = pb * PAGE_SIZE
    full = lambda a: pl.BlockSpec(a.shape, lambda b, pt: (0,) * a.ndim)
    grid_spec = pltpu.PrefetchScalarGridSpec(
        num_scalar_prefetch=1,
        grid=(nbd,),
        in_specs=[pl.BlockSpec((1, N_HEADS_A, QK_PAD), lambda b, pt: (b, 0, 0)),
                  pl.BlockSpec((1, N_HEADS_A, QK_PAD), lambda b, pt: (b, 0, 0)),
                  pl.BlockSpec((1, 1, KV_LORA), lambda b, pt: (b, 0, 0)),
                  full(wp["w_uk_t"]), full(wp["g_kn"]),
                  pl.BlockSpec(memory_space=pl.ANY),
                  pl.BlockSpec(memory_space=pl.ANY)],
        out_specs=pl.BlockSpec((1, N_HEADS_A, KV_LORA), lambda b, pt: (b, 0, 0)),
        scratch_shapes=[pltpu.VMEM((tb, KV_LORA), f32),
                        pltpu.VMEM((tb, KV_LORA), f32),
                        pltpu.VMEM((pb, QK_ROPE, PAGE_SIZE), f32),
                        pltpu.VMEM((pb, QK_ROPE, PAGE_SIZE), f32),
                        pltpu.SemaphoreType.DMA((2, 2)),
                        pltpu.VMEM((2, sub, KV_LORA), bf16),
                        pltpu.VMEM((2, N_HEADS_A * QK_NOPE, sub), f32)],
    )
    return pl.pallas_call(
        functools.partial(_sattn_kernel, layer=layer, pb=pb, sub=sub),
        grid_spec=grid_spec,
        out_shape=jax.ShapeDtypeStruct((nbd, N_HEADS_A, KV_LORA), f32),
        compiler_params=pltpu.CompilerParams(dimension_semantics=("arbitrary",), vmem_limit_bytes=VMEM_LIMIT),
        name="attn_sample",
    )(page_table, q_s, k_s, c_s, wp["w_uk_t"], wp["g_kn"], cache_ckv, cache_krope_t)


def _stail_kernel(x_ref, ctx_ref, sza_ref, min_ref, szm_ref, conv_ref, c0_ref, n0_ref, m0_ref,
                  wuv_ref, cw_ref, cb_ref, wqk_ref, wv_ref, wg_ref, gb_ref, lng_ref, skip_ref, woa_ref, wom_ref,
                  y_ref, c1_ref, n1_ref, m1_ref, conv1_ref, *, bb):
    mixa = []
    for hh in range(N_HEADS_A):
        att = _dot(ctx_ref[hh].astype(bf16), wuv_ref[hh])
        sl = slice(V_HEAD * hh, V_HEAD * (hh + 1))
        mixa.append((att * sza_ref[:, sl].astype(f32)).astype(bf16))
    mixa = jnp.concatenate(mixa, axis=1)

    x = min_ref[...]
    y = cb_ref[...] + x * cw_ref[3:4, :]
    for j in range(CONV_W - 1):
        y = y + conv_ref[j] * cw_ref[j:j + 1, :]
    for j in range(CONV_W - 2):
        conv1_ref[j] = conv_ref[j + 1]
    conv1_ref[CONV_W - 2] = x
    ca = _silu(y)
    cab = ca.astype(bf16)
    xb = x.astype(bf16)
    qs, ks, vs = [], [], []
    for hh in range(N_HEADS_M):
        sl = slice(DH_M * hh, DH_M * (hh + 1))
        qk = _dot(cab[:, sl], wqk_ref[hh])
        qs.append(qk[:, :DH_M])
        ks.append(qk[:, DH_M:])
        vs.append(_dot(xb[:, sl], wv_ref[hh]))
    qkv = jnp.concatenate(qs + ks + vs, axis=1).astype(bf16)
    g = _dot(qkv, wg_ref[...]) + gb_ref[...]
    li = g
    lf = jax.nn.log_sigmoid(pltpu.roll(g, LANES - N_HEADS_M, 1))
    m0 = m0_ref[...]
    a = lf + m0
    mt = jnp.maximum(a, li)
    w_inter = jnp.exp(a - mt)
    w_new = jnp.exp(li - mt)
    emt = jnp.exp(-mt)
    m1_ref[...] = mt

    rowi = lax.broadcasted_iota(jnp.int32, (bb, DH_M), 0)
    rowp = lax.broadcasted_iota(jnp.int32, (LANES, DH_M), 0)
    zpad = jnp.zeros((LANES - bb, DH_M), f32)
    mixm = []
    for hh in range(N_HEADS_M):
        sl = slice(DH_M * hh, DH_M * (hh + 1))
        col = lambda z: jnp.broadcast_to(z[:, hh:hh + 1], (bb, DH_M))
        wi, wn, em = col(w_inter), col(w_new), col(emt)
        qh = qs[hh]
        kh = ks[hh] * K_SCALE_M
        vh = vs[hh]
        qb_ = qh.astype(bf16)
        kw = (kh * wn)
        kwt = jnp.concatenate([kw, zpad], axis=0).T.astype(bf16)
        vb_ = vh.astype(bf16)
        vpad = jnp.concatenate([vh, zpad], axis=0)
        inter = jnp.zeros((bb, DH_M), f32)
        for r in range(bb):
            c0 = c0_ref[r, hh]
            inter = inter + jnp.where(rowi == r, _dot(qb_, c0.astype(bf16)), 0.0)
            upd = _dot(kwt, jnp.where(rowp == r, vpad, 0.0).astype(bf16))
            dec = jnp.broadcast_to(wi[r:r + 1, :], (DH_M, DH_M))
            c1_ref[r, hh] = dec * c0 + upd
        n0 = n0_ref[hh]
        n1_ref[hh] = wi * n0 + kw
        qk_dot = jnp.sum(qb_.astype(f32) * kh.astype(bf16).astype(f32), axis=1, keepdims=True) * wn
        num = wi * inter + qk_dot * vb_.astype(f32)
        den = wi * jnp.sum(qb_.astype(f32) * n0.astype(bf16).astype(f32), axis=1, keepdims=True) + qk_dot
        hv = num / jnp.maximum(jnp.abs(den), em)
        mu = jnp.mean(hv, axis=-1, keepdims=True)
        hc = hv - mu
        var = jnp.mean(hc * hc, axis=-1, keepdims=True)
        hn = hc * lax.rsqrt(var + LN_EPS) * lng_ref[:, sl]
        out = hn + skip_ref[:, sl] * ca[:, sl]
        mixm.append((out * szm_ref[:, sl].astype(f32)).astype(bf16))
    mixm = jnp.concatenate(mixm, axis=1)
    y_ref[...] = x_ref[...] + _dot(mixa, woa_ref[...]) + _dot(mixm, wom_ref[...])


def _sample_tail(x_s, ctx, sza, m_in, szm, conv0, c0, n0, m0, wp, bb):
    nbd = x_s.shape[0]
    full = lambda a: pl.BlockSpec(a.shape, lambda i: (0,) * a.ndim)
    row = lambda w: pl.BlockSpec((bb, w), lambda i: (i, 0))
    weights = (wp["w_uv"], wp["conv_w"], wp["conv_b"], wp["w_qk_m"], wp["w_v_m"], wp["w_gate"], wp["gate_b"],
               wp["mh_norm_g"], wp["skip"], wp["w_out_a"], wp["w_out_m"])
    conv_spec = pl.BlockSpec((CONV_W - 1, bb, D_MLS), lambda i: (0, i, 0))
    c_spec = pl.BlockSpec((bb, N_HEADS_M, DH_M, DH_M), lambda i: (i, 0, 0, 0))
    n_spec = pl.BlockSpec((N_HEADS_M, bb, DH_M), lambda i: (0, i, 0))
    return pl.pallas_call(
        functools.partial(_stail_kernel, bb=bb),
        grid=(nbd // bb,),
        in_specs=[row(D_MODEL),
                  pl.BlockSpec((N_HEADS_A, bb, KV_LORA), lambda i: (0, i, 0)),
                  row(D_ATT), row(D_MLS), row(D_MLS), conv_spec, c_spec, n_spec, row(LANES)]
                 + [full(w) for w in weights],
        out_specs=(row(D_MODEL), c_spec, n_spec, row(LANES), conv_spec),
        out_shape=(jax.ShapeDtypeStruct((nbd, D_MODEL), f32),
                   jax.ShapeDtypeStruct(c0.shape, f32),
                   jax.ShapeDtypeStruct(n0.shape, f32),
                   jax.ShapeDtypeStruct((nbd, LANES), f32),
                   jax.ShapeDtypeStruct(conv0.shape, f32)),
        compiler_params=pltpu.CompilerParams(dimension_semantics=("arbitrary",), vmem_limit_bytes=VMEM_LIMIT),
        name="sample_tail",
    )(x_s, ctx, sza, m_in, szm, conv0, c0, n0, m0, *weights)


def _prep_weights(l, norm_g, w_in, q_norm_g, w_uq, kv_norm_g, w_uk, w_uv, g_qn, g_qr, g_kn, g_kr,
                  conv_w, conv_b, w_q_m, w_k_m, w_v_m, w_gate, b_i, b_f, mh_norm_g, skip, w_out):
    wi = w_in[l]
    o1 = Q_LORA + KV_LORA + QK_ROPE
    w1 = jnp.concatenate([wi[:, :o1], jnp.zeros((D_MODEL, LANES - QK_ROPE), f32)], axis=1)
    w2 = wi[:, o1:]
    wq = w_uq[l].reshape(Q_LORA, N_HEADS_A, QK_NOPE + QK_ROPE)
    wq_rope = jnp.concatenate([wq[:, :, QK_NOPE:], jnp.zeros((Q_LORA, N_HEADS_A, LANES - QK_ROPE), f32)], axis=2)
    wuq = jnp.concatenate([wq[:, :, :QK_NOPE].reshape(Q_LORA, -1), wq_rope.reshape(Q_LORA, -1)], axis=1)
    pad_rope = lambda g: jnp.concatenate([g, jnp.zeros((LANES - QK_ROPE,), f32)])[None, :]
    wuk = w_uk[l].reshape(KV_LORA, N_HEADS_A * QK_NOPE)
    wg = jnp.concatenate([w_gate[l], jnp.zeros((3 * D_MLS, LANES - 2 * N_HEADS_M), f32)], axis=1)
    gate_b = jnp.concatenate([b_i[l], b_f[l], jnp.zeros((LANES - 2 * N_HEADS_M,), f32)])[None, :]
    return {
        "norm_g": norm_g[l][None, :],
        "w1": w1.astype(bf16),
        "w2": w2.astype(bf16),
        "q_norm_g": q_norm_g[l][None, :],
        "w_uq": wuq.astype(bf16),
        "kv_norm_g": kv_norm_g[l][None, :],
        "w_uk": wuk.astype(bf16),
        "w_uk_t": wuk.T.astype(bf16),
        "w_uv": jnp.transpose(w_uv[l], (1, 0, 2)).astype(bf16),
        "g_qn": g_qn[l][None, :],
        "g_qr": pad_rope(g_qr[l]),
        "g_kn": g_kn[l][None, :],
        "g_kr": pad_rope(g_kr[l]),
        "conv_w": conv_w[l],
        "conv_b": conv_b[l][None, :],
        "w_qk_m": jnp.concatenate([w_q_m[l], w_k_m[l]], axis=2).astype(bf16),
        "w_v_m": w_v_m[l].astype(bf16),
        "w_gate": wg.astype(bf16),
        "gate_b": gate_b,
        "mh_norm_g": mh_norm_g[l][None, :],
        "skip": skip[l][None, :],
        "w_out_a": w_out[l][:D_ATT].astype(bf16),
        "w_out_m": w_out[l][D_ATT:].astype(bf16),
    }


def _pick(n, candidates):
    for c in candidates:
        if n % c == 0:
            return c
    raise ValueError(f"no tile size for extent {n}")


def kernel(x_prompt, x_sample, cache_ckv, cache_krope, state_C, state_n, state_m, state_conv, page_table,
           norm_g, w_in, q_norm_g, w_uq, kv_norm_g, w_uk, w_uv, g_qn, g_qr, g_kn, g_kr,
           conv_w, conv_b, w_q_m, w_k_m, w_v_m, w_gate, b_i, b_f, mh_norm_g, skip, w_out):
    nb, s, _ = x_prompt.shape
    nbd, sd, _ = x_sample.shape
    depth = norm_g.shape[0]
    assert sd == 1, "sample path handles one new token per sequence"
    npages = page_table.shape[1]
    past_len = npages * PAGE_SIZE

    cos_p, sin_p = _rope_tables(jnp.arange(s))
    cos_s, sin_s = _rope_tables(jnp.full((nbd,), past_len, jnp.int32))

    tm_p = _pick(s, (512, 256, 128))
    t_att = _pick(s, (256, 128))
    ts_m = _pick(s, (512, 256, 128))
    tm_o = _pick(nb * s, (512, 256, 128))
    assert npages % 2 == 0, "sample attention double-buffers an even number of page blocks"
    pb = _pick(npages // 2, (32, 16, 8, 4, 2, 1))
    sub = _pick(pb * PAGE_SIZE, (512, 256, 128))
    bb = _pick(nbd, (16,))

    cache_krope_t = jnp.swapaxes(cache_krope, 2, 3)

    yp = x_prompt
    ys = x_sample.reshape(1, nbd, D_MODEL)
    outs_p, outs_s = [], []
    for l in range(depth):
        wp = _prep_weights(l, norm_g, w_in, q_norm_g, w_uq, kv_norm_g, w_uk, w_uv, g_qn, g_qr, g_kn, g_kr,
                           conv_w, conv_b, w_q_m, w_k_m, w_v_m, w_gate, b_i, b_f, mh_norm_g, skip, w_out)
        q, k, ckv, ckvb, kr, sza, m_in, szm = _proj(yp, cos_p, sin_p, wp, tm_p)
        mixa = _attn_prompt(q, k, ckvb, sza, wp["w_uv"], t_att)
        mixm, cn, mm, halo = _mlstm_prompt(m_in, szm, wp, ts_m, LANES)
        yp = _out_proj(yp.reshape(nb * s, D_MODEL), mixa.reshape(nb * s, D_ATT), mixm.reshape(nb * s, D_MLS),
                       wp, tm_o).reshape(nb, s, D_MODEL)
        outs_p.append((ckv, kr, cn[..., :DH_M], cn[..., DH_M], mm[:, :N_HEADS_M, 0],
                       halo[:, 8 - (CONV_W - 1):, :]))
        q_s, k_s, ckv_s, _, kr_s, sza_s, m_in_s, szm_s = _proj(ys, cos_s, sin_s, wp, nbd)
        ctx = _attn_sample(page_table, jnp.transpose(q_s[0], (1, 0, 2)), jnp.transpose(k_s[0], (1, 0, 2)),
                           ckv_s.reshape(nbd, 1, KV_LORA), wp, cache_ckv, cache_krope_t, l, pb, sub)
        m0 = jnp.concatenate([state_m[l], jnp.zeros((nbd, LANES - N_HEADS_M), f32)], axis=1)
        to_lead = lambda a: jnp.transpose(a, (1, 0, 2))
        y_s, c1, n1, m1, conv1 = _sample_tail(ys[0], to_lead(ctx), sza_s[0], m_in_s[0], szm_s[0],
                                              to_lead(state_conv[l]), state_C[l], to_lead(state_n[l]), m0, wp, bb)
        ys = y_s.reshape(1, nbd, D_MODEL)
        outs_s.append((ckv_s.reshape(nbd, 1, KV_LORA), kr_s.reshape(nbd, 1, QK_ROPE), c1, to_lead(n1),
                       m1[:, :N_HEADS_M], to_lead(conv1)))
    stk = lambda outs, i: jnp.stack([o[i] for o in outs], axis=0)
    return (yp, ys.reshape(nbd, 1, D_MODEL),
            stk(outs_p, 0), stk(outs_p, 1), stk(outs_p, 2), stk(outs_p, 3), stk(outs_p, 4), stk(outs_p, 5),
            stk(outs_s, 0), stk(outs_s, 1), stk(outs_s, 2), stk(outs_s, 3), stk(outs_s, 4), stk(outs_s, 5))
```

```python
import functools

import jax
import jax.numpy as jnp
from jax import lax
from jax.experimental import pallas as pl
from jax.experimental.pallas import tpu as pltpu

f32 = jnp.float32
bf16 = jnp.bfloat16

D_MODEL = 1024
D_ATT = 512
D_MLS = 512
N_HEADS_A = 4
QK_NOPE = 128
QK_ROPE = 64
V_HEAD = 128
Q_LORA = 384
KV_LORA = 256
ROPE_BASE = 10000.0
N_HEADS_M = 4
DH_M = 128
CONV_W = 4
PAGE_SIZE = 128
EPS = 1e-6
LN_EPS = 1e-5
ATT_SCALE = (QK_NOPE + QK_ROPE) ** -0.5
LOG2E = 1.4426950408889634
Q_SCALE = ATT_SCALE * LOG2E
K_SCALE_M = DH_M ** -0.5
AHEAD = 4
PROJ_ROWS = 128

LANES = 128
QK_PAD = 256
VMEM_LIMIT = 56 * 1024 * 1024

NT_DIMS = (((1,), (1,)), ((), ()))


def _rms(x, n):
    ms = jnp.sum(x * x, axis=-1, keepdims=True) * (1.0 / n)
    return x * lax.rsqrt(ms + EPS)


def _silu(x):
    return x * jax.nn.sigmoid(x)


def _dot(a, b):
    return jnp.dot(a, b, preferred_element_type=f32)


def _dot_nt(a, b):
    return lax.dot_general(a, b, NT_DIMS, preferred_element_type=f32)


def _rep_lanes(x, n):
    return x if n == 1 else jnp.concatenate([x] * n, axis=1)


def _rope_table_kernel(ang_ref, cos_ref, sin_ref):
    ang = ang_ref[...]
    lane = lax.broadcasted_iota(jnp.int32, ang.shape, 1)
    cos_ref[...] = jnp.cos(ang)
    s = jnp.sin(ang)
    sin_ref[...] = jnp.where((lane % QK_ROPE) < QK_ROPE // 2, -s, s)


def _rope_tables(pos):
    half = QK_ROPE // 2
    inv = 1.0 / (ROPE_BASE ** (jnp.arange(0, QK_ROPE, 2, dtype=f32) / QK_ROPE))
    ang = pos.astype(f32)[:, None] * inv[None, :]
    ang = jnp.tile(ang, (1, LANES // half))
    s = ang.shape[0]
    return pl.pallas_call(
        _rope_table_kernel,
        out_shape=(jax.ShapeDtypeStruct((s, LANES), f32), jax.ShapeDtypeStruct((s, LANES), f32)),
        name="rope_tables",
    )(ang)


def _proj_kernel(x_ref, cos_ref, sin_ref, ng_ref, w1_ref, w2_ref, qg_ref, wuq_ref, kvg_ref, wuk_ref,
                 gqn_ref, gqr_ref, gkn_ref, gkr_ref,
                 qt_ref, k_ref, ckv_ref, ct_ref, kr_ref, sza_ref, min_ref, szm_ref):
    tm = x_ref.shape[1]
    rows = min(tm, PROJ_ROWS)
    tq = qt_ref.shape[4]
    tk = ct_ref.shape[3]
    lane = lax.broadcasted_iota(jnp.int32, (rows, LANES), 1)
    first_half = (lane % QK_ROPE) < QK_ROPE // 2

    for r0 in range(0, tm, rows):
        rs = slice(r0, r0 + rows)
        x = x_ref[0, rs, :]
        h = (_rms(x, D_MODEL) * ng_ref[...]).astype(bf16)
        p1 = _dot(h, w1_ref[...])
        p2 = _dot(h, w2_ref[...])
        cos = cos_ref[rs, :]
        sin = sin_ref[rs, :]

        def rope(xp):
            sw = jnp.where(first_half, pltpu.roll(xp, LANES - QK_ROPE // 2, 1), pltpu.roll(xp, QK_ROPE // 2, 1))
            return xp * cos + sw * sin

        ql = (_rms(p1[:, :Q_LORA], Q_LORA) * qg_ref[...]).astype(bf16)
        qf = _dot(ql, wuq_ref[...])
        for hh in range(N_HEADS_A):
            qn = _rms(qf[:, LANES * hh:LANES * (hh + 1)], QK_NOPE) * gqn_ref[...]
            o = N_HEADS_A * QK_NOPE + LANES * hh
            qr = rope(_rms(qf[:, o:o + LANES], QK_ROPE) * gqr_ref[...])
            qh = jnp.concatenate([qn, qr], axis=1) * Q_SCALE
            qt_ref[0, hh, r0 // tq, :, r0 % tq:r0 % tq + rows] = qh.T.astype(bf16)

        c = _rms(p1[:, Q_LORA:Q_LORA + KV_LORA], KV_LORA) * kvg_ref[...]
        ckv_ref[0, rs, :] = c
        cb = c.astype(bf16)
        ct_ref[0, r0 // tk, :, r0 % tk:r0 % tk + rows] = c.T.astype(bf16)
        kn = _dot(cb, wuk_ref[...])
        o = Q_LORA + KV_LORA
        krp = rope(_rms(p1[:, o:o + LANES], QK_ROPE) * gkr_ref[...])
        kr_ref[0, rs, :] = krp[:, :QK_ROPE]
        krb = krp.astype(bf16)
        for hh in range(N_HEADS_A):
            knh = _rms(kn[:, LANES * hh:LANES * (hh + 1)], QK_NOPE) * gkn_ref[...]
            k_ref[0, hh, rs, 0:LANES] = knh.astype(bf16)
            k_ref[0, hh, rs, LANES:QK_PAD] = krb

        sza_ref[0, rs, :] = _silu(p2[:, :D_ATT]).astype(bf16)
        min_ref[0, rs, :] = p2[:, D_ATT:D_ATT + D_MLS]
        szm_ref[0, rs, :] = _silu(p2[:, D_ATT + D_MLS:]).astype(bf16)


def _proj(x, cos, sin, wp, tm, tq, tk):
    nb, s, _ = x.shape
    grid = (nb, s // tm)
    full = lambda a: pl.BlockSpec(a.shape, lambda b, i: (0,) * a.ndim)
    tok = lambda w: pl.BlockSpec((1, tm, w), lambda b, i: (b, i, 0))
    head = pl.BlockSpec((1, N_HEADS_A, tm, QK_PAD), lambda b, i: (b, 0, i, 0))
    head_t = pl.BlockSpec((1, N_HEADS_A, tm // tq, QK_PAD, tq), lambda b, i: (b, 0, i, 0, 0))
    lat_t = pl.BlockSpec((1, tm // tk, KV_LORA, tk), lambda b, i: (b, i, 0, 0))
    tab = pl.BlockSpec((tm, LANES), lambda b, i: (i, 0))
    weights = (wp["norm_g"], wp["w1"], wp["w2"], wp["q_norm_g"], wp["w_uq"], wp["kv_norm_g"], wp["w_uk"],
               wp["g_qn"], wp["g_qr"], wp["g_kn"], wp["g_kr"])
    out_shape = (
        jax.ShapeDtypeStruct((nb, N_HEADS_A, s // tq, QK_PAD, tq), bf16),
        jax.ShapeDtypeStruct((nb, N_HEADS_A, s, QK_PAD), bf16),
        jax.ShapeDtypeStruct((nb, s, KV_LORA), f32),
        jax.ShapeDtypeStruct((nb, s // tk, KV_LORA, tk), bf16),
        jax.ShapeDtypeStruct((nb, s, QK_ROPE), f32),
        jax.ShapeDtypeStruct((nb, s, D_ATT), bf16),
        jax.ShapeDtypeStruct((nb, s, D_MLS), f32),
        jax.ShapeDtypeStruct((nb, s, D_MLS), bf16),
    )
    out_specs = (head_t, head, tok(KV_LORA), lat_t, tok(QK_ROPE), tok(D_ATT), tok(D_MLS), tok(D_MLS))
    return pl.pallas_call(
        _proj_kernel,
        grid=grid,
        in_specs=[tok(D_MODEL), tab, tab] + [full(w) for w in weights],
        out_specs=out_specs,
        out_shape=out_shape,
        compiler_params=pltpu.CompilerParams(dimension_semantics=("arbitrary", "arbitrary"),
                                             vmem_limit_bytes=VMEM_LIMIT),
        name="in_proj",
    )(x, cos, sin, *weights)


def _attn_kernel(qt_ref, k_ref, ct_ref, sza_ref, wuv_ref, o_ref, m_sc, l_sc, acc_sc, s_sc, *, tq, tk):
    i = pl.program_id(1)
    ratio = tq // tk
    kv_idx = lax.broadcasted_iota(jnp.int32, (tk, tq), 0)
    q_idx = lax.broadcasted_iota(jnp.int32, (tk, tq), 1)

    m_sc[...] = jnp.full(m_sc.shape, -jnp.inf, f32)
    l_sc[...] = jnp.zeros(l_sc.shape, f32)
    acc_sc[...] = jnp.zeros(acc_sc.shape, f32)

    def step(j, diag):
        start = pl.multiple_of(j * tk, tk)
        ctj = ct_ref[0, j]

        def scores(hh):
            s = _dot(k_ref[0, hh, pl.ds(start, tk), :], qt_ref[0, hh, 0])
            s_sc[hh] = s if diag is None else jnp.where(kv_idx + diag * tk <= q_idx, s, -jnp.inf)

        def update(hh):
            s = s_sc[hh]
            m_old = m_sc[hh]
            m_new = jnp.maximum(m_old, jnp.max(s, axis=0, keepdims=True))
            alpha = jnp.exp2(m_old - m_new)
            p = jnp.exp2(s - m_new[0:1, :])
            l_sc[hh] = alpha * l_sc[hh] + jnp.sum(p, axis=0, keepdims=True)
            acc_sc[hh] = alpha[0:1, :] * acc_sc[hh] + _dot(ctj, p.astype(bf16))
            m_sc[hh] = m_new

        for hh in range(min(AHEAD, N_HEADS_A)):
            scores(hh)
        for hh in range(N_HEADS_A):
            if hh + AHEAD < N_HEADS_A:
                scores(hh + AHEAD)
            update(hh)

    def body(j, carry):
        step(j, None)
        return carry

    lax.fori_loop(0, ratio * i, body, 0)
    for d in range(ratio):
        step(ratio * i + d, d)

    for hh in range(N_HEADS_A):
        inv = 1.0 / l_sc[hh]
        ctx = (acc_sc[hh] * inv[0:1, :]).T.astype(bf16)
        att = _dot(ctx, wuv_ref[hh])
        sl = slice(V_HEAD * hh, V_HEAD * (hh + 1))
        o_ref[0, :, sl] = (att * sza_ref[0, :, sl].astype(f32)).astype(bf16)


def _attn_prompt(qt, k, ct, sza, wuv, tq, tk):
    nb, _, s, _ = k.shape
    assert tq % tk == 0
    assert qt.shape == (nb, N_HEADS_A, s // tq, QK_PAD, tq) and ct.shape == (nb, s // tk, KV_LORA, tk)
    return pl.pallas_call(
        functools.partial(_attn_kernel, tq=tq, tk=tk),
        grid=(nb, s // tq),
        in_specs=[
            pl.BlockSpec((1, N_HEADS_A, 1, QK_PAD, tq), lambda b, i: (b, 0, i, 0, 0)),
            pl.BlockSpec((1, N_HEADS_A, s, QK_PAD), lambda b, i: (b, 0, 0, 0)),
            pl.BlockSpec((1, s // tk, KV_LORA, tk), lambda b, i: (b, 0, 0, 0)),
            pl.BlockSpec((1, tq, D_ATT), lambda b, i: (b, i, 0)),
            pl.BlockSpec(wuv.shape, lambda b, i: (0, 0, 0)),
        ],
        out_specs=pl.BlockSpec((1, tq, D_ATT), lambda b, i: (b, i, 0)),
        out_shape=jax.ShapeDtypeStruct((nb, s, D_ATT), bf16),
        scratch_shapes=[pltpu.VMEM((N_HEADS_A, 8, tq), f32),
                        pltpu.VMEM((N_HEADS_A, 8, tq), f32),
                        pltpu.VMEM((N_HEADS_A, KV_LORA, tq), f32),
                        pltpu.VMEM((N_HEADS_A, tk, tq), f32)],
        compiler_params=pltpu.CompilerParams(dimension_semantics=("arbitrary", "arbitrary"),
                                             vmem_limit_bytes=VMEM_LIMIT),
        name="attn_prompt",
    )(qt, k, ct, sza, wuv)


def _scan_lanes(x, op, fill, length):
    lane = lax.broadcasted_iota(jnp.int32, x.shape, 1)
    d = 1
    while d < length:
        x = op(x, jnp.where(lane >= d, pltpu.roll(x, d, 1), fill))
        d *= 2
    return x


def _mlstm_kernel(min_ref, szm_ref, cw_ref, cb_ref, wqk_ref, wv_ref, wg_ref, gb_ref, lng_ref, skip_ref, sel_ref,
                  mix_ref, cn_ref, m_ref, halo_ref,
                  xbuf, ca_sc, qkv_sc, ks_sc, cn_sc, m_sc, *, ts, lc):
    si = pl.program_id(1)
    ns = pl.num_programs(1)
    nc = ts // lc

    @pl.when(si == 0)
    def _():
        xbuf[0:8, :] = jnp.zeros((8, D_MLS), f32)
        cn_sc[...] = jnp.zeros(cn_sc.shape, f32)
        m_sc[...] = jnp.zeros(m_sc.shape, f32)

    x = min_ref[0]
    xbuf[8:8 + ts, :] = x
    y = cb_ref[...] + xbuf[8:8 + ts, :] * cw_ref[3:4, :]
    for j in range(CONV_W - 1):
        y = y + xbuf[5 + j:5 + j + ts, :] * cw_ref[j:j + 1, :]
    xbuf[0:8, :] = xbuf[ts:ts + 8, :]
    ca = _silu(y)
    ca_sc[...] = ca
    cab = ca.astype(bf16)
    xb = x.astype(bf16)
    for hh in range(N_HEADS_M):
        sl = slice(DH_M * hh, DH_M * (hh + 1))
        qk = _dot(cab[:, sl], wqk_ref[hh])
        qkv_sc[:, sl] = qk[:, :DH_M].astype(bf16)
        kh = qk[:, DH_M:]
        qkv_sc[:, D_MLS + DH_M * hh:D_MLS + DH_M * (hh + 1)] = kh.astype(bf16)
        ks_sc[:, sl] = (kh * K_SCALE_M).astype(bf16)
        qkv_sc[:, 2 * D_MLS + DH_M * hh:2 * D_MLS + DH_M * (hh + 1)] = _dot(xb[:, sl], wv_ref[hh]).astype(bf16)

    g = _dot(qkv_sc[...], wg_ref[...]) + gb_ref[...]
    gt = g.T[0:8, :]
    head_row = lax.broadcasted_iota(jnp.int32, gt.shape, 0) < N_HEADS_M
    li_all = jnp.where(head_row, gt, 0.0)
    lf_all = jnp.where(head_row, jax.nn.log_sigmoid(pltpu.roll(gt, 4, 0)), 0.0)

    row = lax.broadcasted_iota(jnp.int32, (lc, lc), 0)
    col = lax.broadcasted_iota(jnp.int32, (lc, lc), 1)
    causal = col <= row
    ones_b = jnp.ones((lc, LANES), bf16)

    for c in range(nc):
        cs = slice(lc * c, lc * (c + 1))
        li = li_all[:, cs]
        lf = lf_all[:, cs]
        m0 = m_sc[...]
        b = _scan_lanes(lf, jnp.add, 0.0, lc)
        u = li - b
        gmax = jnp.maximum(_rep_lanes(m0, lc // LANES), _scan_lanes(u, jnp.maximum, -jnp.inf, lc))
        mt = b + gmax
        g_last = jnp.maximum(m0, jnp.max(u, axis=1, keepdims=True))
        m_new = jnp.sum(lf, axis=1, keepdims=True) + g_last
        decay = jnp.exp(m0 - g_last)
        w_s = jnp.exp(u - _rep_lanes(g_last, lc // LANES))
        r = jnp.concatenate([gmax, mt], axis=0)
        r_hi = r.astype(bf16).astype(f32)
        r_mid = (r - r_hi).astype(bf16).astype(f32)
        r_lo = r - r_hi - r_mid
        stack = jnp.concatenate([r_hi, r_mid, r_lo, jnp.zeros((lc - 48, lc), f32)], axis=0)
        cols = _dot(stack.T.astype(bf16), sel_ref[...])

        for hh in range(N_HEADS_M):
            sl = slice(DH_M * hh, DH_M * (hh + 1))
            g_col = cols[:, LANES * hh:LANES * (hh + 1)]
            mt_col = cols[:, LANES * (N_HEADS_M + hh):LANES * (N_HEADS_M + hh + 1)]
            qc = qkv_sc[cs, sl]
            kc = ks_sc[cs, sl]
            vc = qkv_sc[cs, 2 * D_MLS + DH_M * hh:2 * D_MLS + DH_M * (hh + 1)]
            s_qk = _dot_nt(qc, kc)
            wm = jnp.where(causal, jnp.exp(u[hh:hh + 1, :] - g_col), 0.0)
            p = (s_qk * wm).astype(bf16)
            v_aug = jnp.concatenate([vc, ones_b], axis=1)
            intra = _dot(p, v_aug)
            inter = _dot(qc, cn_sc[hh].astype(bf16))
            w_inter = jnp.exp(m0[hh:hh + 1, :] - g_col)
            tot = _rep_lanes(w_inter, 2) * inter + intra
            num = tot[:, :DH_M]
            den = tot[:, DH_M:]
            hv = num / jnp.maximum(jnp.abs(den), jnp.exp(-mt_col))
            mu = jnp.mean(hv, axis=-1, keepdims=True)
            hc = hv - mu
            var = jnp.mean(hc * hc, axis=-1, keepdims=True)
            hn = hc * lax.rsqrt(var + LN_EPS) * lng_ref[:, sl]
            out = hn + skip_ref[:, sl] * ca_sc[cs, sl]
            mix_ref[0, cs, sl] = (out * szm_ref[0, cs, sl].astype(f32)).astype(bf16)
            ktw = (kc.astype(f32).T * w_s[hh:hh + 1, :]).astype(bf16)
            upd = _dot(ktw, v_aug)
            dec = jnp.broadcast_to(decay[hh:hh + 1, :], (DH_M, LANES))
            cn_sc[hh] = _rep_lanes(dec, 2) * cn_sc[hh] + upd
        m_sc[...] = m_new

    @pl.when(si == ns - 1)
    def _():
        cn_ref[0] = cn_sc[...]
        m_ref[0] = m_sc[...]
        halo_ref[0] = xbuf[0:8, :]


def _column_selector():
    rows = lax.broadcasted_iota(jnp.int32, (LANES, 2 * N_HEADS_M * LANES), 0)
    block = lax.broadcasted_iota(jnp.int32, (LANES, 2 * N_HEADS_M * LANES), 1) // LANES
    src = 8 * (block // N_HEADS_M) + block % N_HEADS_M
    return ((rows < 48) & (rows % 16 == src)).astype(bf16)


def _mlstm_prompt(m_in, szm, wp, ts, lc):
    nb, s, _ = m_in.shape
    full = lambda a: pl.BlockSpec(a.shape, lambda b, i: (0,) * a.ndim)
    tok = pl.BlockSpec((1, ts, D_MLS), lambda b, i: (b, i, 0))
    assert lc == LANES, "one chunk spans exactly one vreg of lanes"
    weights = (wp["conv_w"], wp["conv_b"], wp["w_qk_m"], wp["w_v_m"], wp["w_gate"], wp["gate_b"],
               wp["mh_norm_g"], wp["skip"], _column_selector())
    return pl.pallas_call(
        functools.partial(_mlstm_kernel, ts=ts, lc=lc),
        grid=(nb, s // ts),
        in_specs=[tok, tok] + [full(w) for w in weights],
        out_specs=(tok,
                   pl.BlockSpec((1, N_HEADS_M, DH_M, 2 * DH_M), lambda b, i: (b, 0, 0, 0)),
                   pl.BlockSpec((1, 8, LANES), lambda b, i: (b, 0, 0)),
                   pl.BlockSpec((1, 8, D_MLS), lambda b, i: (b, 0, 0))),
        out_shape=(jax.ShapeDtypeStruct((nb, s, D_MLS), bf16),
                   jax.ShapeDtypeStruct((nb, N_HEADS_M, DH_M, 2 * DH_M), f32),
                   jax.ShapeDtypeStruct((nb, 8, LANES), f32),
                   jax.ShapeDtypeStruct((nb, 8, D_MLS), f32)),
        scratch_shapes=[pltpu.VMEM((ts + 8, D_MLS), f32),
                        pltpu.VMEM((ts, D_MLS), f32),
                        pltpu.VMEM((ts, 3 * D_MLS), bf16),
                        pltpu.VMEM((ts, D_MLS), bf16),
                        pltpu.VMEM((N_HEADS_M, DH_M, 2 * DH_M), f32),
                        pltpu.VMEM((8, LANES), f32)],
        compiler_params=pltpu.CompilerParams(dimension_semantics=("arbitrary", "arbitrary"),
                                             vmem_limit_bytes=VMEM_LIMIT),
        name="mlstm_prompt",
    )(m_in, szm, *weights)


def _out_kernel(x_ref, ma_ref, mm_ref, wa_ref, wm_ref, y_ref):
    y_ref[...] = x_ref[...] + _dot(ma_ref[...], wa_ref[...]) + _dot(mm_ref[...], wm_ref[...])


def _out_proj(x2, mixa, mixm, wp, tm):
    n = x2.shape[0]
    full = lambda a: pl.BlockSpec(a.shape, lambda i: (0,) * a.ndim)
    return pl.pallas_call(
        _out_kernel,
        grid=(n // tm,),
        in_specs=[pl.BlockSpec((tm, D_MODEL), lambda i: (i, 0)),
                  pl.BlockSpec((tm, D_ATT), lambda i: (i, 0)),
                  pl.BlockSpec((tm, D_MLS), lambda i: (i, 0)),
                  full(wp["w_out_a"]), full(wp["w_out_m"])],
        out_specs=pl.BlockSpec((tm, D_MODEL), lambda i: (i, 0)),
        out_shape=jax.ShapeDtypeStruct((n, D_MODEL), f32),
        compiler_params=pltpu.CompilerParams(dimension_semantics=("arbitrary",), vmem_limit_bytes=VMEM_LIMIT),
        name="out_proj",
    )(x2, mixa, mixm, wp["w_out_a"], wp["w_out_m"])


def _sattn_kernel(pt_ref, q_ref, knew_ref, cnew_ref, wukt_ref, gkn_ref, ckv_hbm, krt_hbm, ctx_ref,
                  cbuf0, cbuf1, kbuf0, kbuf1, sem, cb_sc, knt_sc, lhs_sc, *, layer, pb, sub):
    b = pl.program_id(0)
    nbatch = pl.num_programs(0)
    npages = pt_ref.shape[1]
    nblk = npages // pb
    tb = pb * PAGE_SIZE
    nsub = tb // sub
    sub_pages = sub // PAGE_SIZE
    cbufs = (cbuf0, cbuf1)
    kbufs = (kbuf0, kbuf1)

    def page_copies(bb, blk, slot):
        latent, ropek = [], []
        for p in range(pb):
            page = pt_ref[bb, blk * pb + p]
            rows = pl.ds(p * PAGE_SIZE, PAGE_SIZE)
            latent.append(pltpu.make_async_copy(ckv_hbm.at[layer, page], cbufs[slot].at[rows], sem.at[slot, 0]))
            ropek.append(pltpu.make_async_copy(krt_hbm.at[layer, page], kbufs[slot].at[p], sem.at[slot, 1]))
        return latent + ropek

    def start_block(bb, blk, slot):
        for cp in page_copies(bb, blk, slot):
            cp.start()

    def wait_block(bb, blk, slot):
        for cp in page_copies(bb, blk, slot):
            cp.wait()

    @pl.when(b == 0)
    def _():
        start_block(0, 0, 0)

    q = q_ref[0]
    qg = q[:, :QK_NOPE].astype(f32) * gkn_ref[...]
    rowq = lax.broadcasted_iota(jnp.int32, (16, QK_NOPE), 0)
    qabs = jnp.zeros((16, KV_LORA), f32)
    for hh in range(N_HEADS_A):
        lhs = jnp.where(rowq == hh, jnp.broadcast_to(qg[hh:hh + 1, :], (16, QK_NOPE)), 0.0).astype(bf16)
        qabs = qabs + _dot(lhs, wukt_ref[QK_NOPE * hh:QK_NOPE * (hh + 1), :])
    nk = N_HEADS_A * QK_NOPE
    lhs_sc[0:nk, :] = wukt_ref[...]
    lhs_sc[nk:nk + 16, :] = qabs.astype(bf16)
    qr16 = jnp.concatenate([q[:, QK_NOPE:QK_NOPE + QK_ROPE],
                            jnp.zeros((16 - N_HEADS_A, QK_ROPE), bf16)], axis=0)

    row16 = lax.broadcasted_iota(jnp.int32, (16, sub), 0)

    def matmuls(slot, si):
        cb = cbufs[slot][sub * si:sub * (si + 1), :].astype(bf16)
        cb_sc[si % 2] = cb
        knt_sc[si % 2] = _dot_nt(lhs_sc[...], cb)
        srope = jnp.concatenate(
            [_dot(qr16, kbufs[slot][sub_pages * si + p].astype(bf16)) for p in range(sub_pages)],
            axis=1)
        return srope

    def softmax_update(si, srope, carry):
        num = knt_sc[si % 2, nk:nk + 16, :]
        m, l, acc = carry
        rnorm = jnp.zeros((16, sub), f32)
        for hh in range(N_HEADS_A):
            kh = knt_sc[si % 2, QK_NOPE * hh:QK_NOPE * (hh + 1), :]
            ss = jnp.sum(kh * kh, axis=0, keepdims=True)
            rnorm = jnp.where(row16 == hh, lax.rsqrt(ss * (1.0 / QK_NOPE) + EPS), rnorm)
        s = num * rnorm + srope
        m_new = jnp.maximum(m, jnp.max(s, axis=1, keepdims=True))
        alpha = jnp.exp2(m - m_new)
        p = jnp.exp2(s - _rep_lanes(m_new, sub // LANES))
        l = alpha * l + jnp.sum(p, axis=1, keepdims=True)
        acc = _rep_lanes(alpha, KV_LORA // LANES) * acc + _dot(p.astype(bf16), cb_sc[si % 2])
        return m_new, l, acc

    def compute(slot, carry):
        staged = matmuls(slot, 0)
        for si in range(nsub):
            nxt = matmuls(slot, si + 1) if si + 1 < nsub else None
            carry = softmax_update(si, staged, carry)
            staged = nxt
        return carry

    def pair(i, carry):
        blk = 2 * i
        wait_block(b, blk, 0)
        start_block(b, blk + 1, 1)
        carry = compute(0, carry)
        wait_block(b, blk + 1, 1)
        wraps = blk + 2 == nblk
        start_block(jnp.where(wraps, jnp.minimum(b + 1, nbatch - 1), b), jnp.where(wraps, 0, blk + 2), 0)
        return compute(1, carry)

    init = (jnp.full((16, LANES), -jnp.inf, f32), jnp.zeros((16, LANES), f32), jnp.zeros((16, KV_LORA), f32))
    m_old, l_old, acc_old = lax.fori_loop(0, nblk // 2, pair, init)

    @pl.when(b == nbatch - 1)
    def _():
        wait_block(b, 0, 0)

    kn = knew_ref[0].astype(f32)
    s_new = jnp.sum(q.astype(f32) * kn, axis=1, keepdims=True)
    s_new = jnp.concatenate([jnp.broadcast_to(s_new, (N_HEADS_A, LANES)),
                             jnp.zeros((16 - N_HEADS_A, LANES), f32)], axis=0)
    m_new = jnp.maximum(m_old, s_new)
    alpha = jnp.exp2(m_old - m_new)
    p_new = jnp.exp2(s_new - m_new)
    l = alpha * l_old + p_new
    acc = _rep_lanes(alpha, 2) * acc_old + _rep_lanes(p_new, 2) * cnew_ref[0]
    ctx = acc * _rep_lanes(1.0 / l, 2)
    ctx_ref[0] = ctx[0:N_HEADS_A, :]


def _attn_sample(page_table, q_s, k_s, c_s, wp, cache_ckv, cache_krope_t, layer, pb, sub):
    nbd = q_s.shape[0]
    tb = pb * PAGE_SIZE
    full = lambda a: pl.BlockSpec(a.shape, lambda b, pt: (0,) * a.ndim)
    grid_spec = pltpu.PrefetchScalarGridSpec(
        num_scalar_prefetch=1,
        grid=(nbd,),
        in_specs=[pl.BlockSpec((1, N_HEADS_A, QK_PAD), lambda b, pt: (b, 0, 0)),
                  pl.BlockSpec((1, N_HEADS_A, QK_PAD), lambda b, pt: (b, 0, 0)),
                  pl.BlockSpec((1, 1, KV_LORA), lambda b, pt: (b, 0, 0)),
                  full(wp["w_uk_t"]), full(wp["g_kn"]),
                  pl.BlockSpec(memory_space=pl.ANY),
                  pl.BlockSpec(memory_space=pl.ANY)],
        out_specs=pl.BlockSpec((1, N_HEADS_A, KV_LORA), lambda b, pt: (b, 0, 0)),
        scratch_shapes=[pltpu.VMEM((tb, KV_LORA), f32),
                        pltpu.VMEM((tb, KV_LORA), f32),
                        pltpu.VMEM((pb, QK_ROPE, PAGE_SIZE), f32),
                        pltpu.VMEM((pb, QK_ROPE, PAGE_SIZE), f32),
                        pltpu.SemaphoreType.DMA((2, 2)),
                        pltpu.VMEM((2, sub, KV_LORA), bf16),
                        pltpu.VMEM((2, N_HEADS_A * QK_NOPE + 16, sub), f32),
                        pltpu.VMEM((N_HEADS_A * QK_NOPE + 16, KV_LORA), bf16)],
    )
    return pl.pallas_call(
        functools.partial(_sattn_kernel, layer=layer, pb=pb, sub=sub),
        grid_spec=grid_spec,
        out_shape=jax.ShapeDtypeStruct((nbd, N_HEADS_A, KV_LORA), f32),
        compiler_params=pltpu.CompilerParams(dimension_semantics=("arbitrary",), vmem_limit_bytes=VMEM_LIMIT),
        name="attn_sample",
    )(page_table, q_s, k_s, c_s, wp["w_uk_t"], wp["g_kn"], cache_ckv, cache_krope_t)


def _stail_kernel(x_ref, ctx_ref, sza_ref, min_ref, szm_ref, conv_ref, c0_ref, n0_ref, m0_ref,
                  wuv_ref, cw_ref, cb_ref, wqk_ref, wv_ref, wg_ref, gb_ref, lng_ref, skip_ref, woa_ref, wom_ref,
                  y_ref, c1_ref, n1_ref, m1_ref, conv1_ref, *, bb):
    mixa = []
    for hh in range(N_HEADS_A):
        att = _dot(ctx_ref[hh].astype(bf16), wuv_ref[hh])
        sl = slice(V_HEAD * hh, V_HEAD * (hh + 1))
        mixa.append((att * sza_ref[:, sl].astype(f32)).astype(bf16))
    mixa = jnp.concatenate(mixa, axis=1)

    x = min_ref[...]
    y = cb_ref[...] + x * cw_ref[3:4, :]
    for j in range(CONV_W - 1):
        y = y + conv_ref[j] * cw_ref[j:j + 1, :]
    for j in range(CONV_W - 2):
        conv1_ref[j] = conv_ref[j + 1]
    conv1_ref[CONV_W - 2] = x
    ca = _silu(y)
    cab = ca.astype(bf16)
    xb = x.astype(bf16)
    qs, ks, vs = [], [], []
    for hh in range(N_HEADS_M):
        sl = slice(DH_M * hh, DH_M * (hh + 1))
        qk = _dot(cab[:, sl], wqk_ref[hh])
        qs.append(qk[:, :DH_M])
        ks.append(qk[:, DH_M:])
        vs.append(_dot(xb[:, sl], wv_ref[hh]))
    qkv = jnp.concatenate(qs + ks + vs, axis=1).astype(bf16)
    g = _dot(qkv, wg_ref[...]) + gb_ref[...]
    li = g
    lf = jax.nn.log_sigmoid(pltpu.roll(g, LANES - N_HEADS_M, 1))
    m0 = m0_ref[...]
    a = lf + m0
    mt = jnp.maximum(a, li)
    w_inter = jnp.exp(a - mt)
    w_new = jnp.exp(li - mt)
    emt = jnp.exp(-mt)
    m1_ref[...] = mt

    rowi = lax.broadcasted_iota(jnp.int32, (bb, DH_M), 0)
    rowp = lax.broadcasted_iota(jnp.int32, (LANES, DH_M), 0)
    zpad = jnp.zeros((LANES - bb, DH_M), f32)
    mixm = []
    for hh in range(N_HEADS_M):
        sl = slice(DH_M * hh, DH_M * (hh + 1))
        col = lambda z: jnp.broadcast_to(z[:, hh:hh + 1], (bb, DH_M))
        wi, wn, em = col(w_inter), col(w_new), col(emt)
        qh = qs[hh]
        kh = ks[hh] * K_SCALE_M
        vh = vs[hh]
        qb_ = qh.astype(bf16)
        kw = (kh * wn)
        kwt = jnp.concatenate([kw, zpad], axis=0).T.astype(bf16)
        vb_ = vh.astype(bf16)
        vpad = jnp.concatenate([vh, zpad], axis=0)
        inter = jnp.zeros((bb, DH_M), f32)
        for r in range(bb):
            c0 = c0_ref[r, hh]
            inter = inter + jnp.where(rowi == r, _dot(qb_, c0.astype(bf16)), 0.0)
            upd = _dot(kwt, jnp.where(rowp == r, vpad, 0.0).astype(bf16))
            dec = jnp.broadcast_to(wi[r:r + 1, :], (DH_M, DH_M))
            c1_ref[r, hh] = dec * c0 + upd
        n0 = n0_ref[hh]
        n1_ref[hh] = wi * n0 + kw
        qk_dot = jnp.sum(qb_.astype(f32) * kh.astype(bf16).astype(f32), axis=1, keepdims=True) * wn
        num = wi * inter + qk_dot * vb_.astype(f32)
        den = wi * jnp.sum(qb_.astype(f32) * n0.astype(bf16).astype(f32), axis=1, keepdims=True) + qk_dot
        hv = num / jnp.maximum(jnp.abs(den), em)
        mu = jnp.mean(hv, axis=-1, keepdims=True)
        hc = hv - mu
        var = jnp.mean(hc * hc, axis=-1, keepdims=True)
        hn = hc * lax.rsqrt(var + LN_EPS) * lng_ref[:, sl]
        out = hn + skip_ref[:, sl] * ca[:, sl]
        mixm.append((out * szm_ref[:, sl].astype(f32)).astype(bf16))
    mixm = jnp.concatenate(mixm, axis=1)
    y_ref[...] = x_ref[...] + _dot(mixa, woa_ref[...]) + _dot(mixm, wom_ref[...])


def _sample_tail(x_s, ctx, sza, m_in, szm, conv0, c0, n0, m0, wp, bb):
    nbd = x_s.shape[0]
    full = lambda a: pl.BlockSpec(a.shape, lambda i: (0,) * a.ndim)
    row = lambda w: pl.BlockSpec((bb, w), lambda i: (i, 0))
    weights = (wp["w_uv"], wp["conv_w"], wp["conv_b"], wp["w_qk_m"], wp["w_v_m"], wp["w_gate"], wp["gate_b"],
               wp["mh_norm_g"], wp["skip"], wp["w_out_a"], wp["w_out_m"])
    conv_spec = pl.BlockSpec((CONV_W - 1, bb, D_MLS), lambda i: (0, i, 0))
    c_spec = pl.BlockSpec((bb, N_HEADS_M, DH_M, DH_M), lambda i: (i, 0, 0, 0))
    n_spec = pl.BlockSpec((N_HEADS_M, bb, DH_M), lambda i: (0, i, 0))
    return pl.pallas_call(
        functools.partial(_stail_kernel, bb=bb),
        grid=(nbd // bb,),
        in_specs=[row(D_MODEL),
                  pl.BlockSpec((N_HEADS_A, bb, KV_LORA), lambda i: (0, i, 0)),
                  row(D_ATT), row(D_MLS), row(D_MLS), conv_spec, c_spec, n_spec, row(LANES)]
                 + [full(w) for w in weights],
        out_specs=(row(D_MODEL), c_spec, n_spec, row(LANES), conv_spec),
        out_shape=(jax.ShapeDtypeStruct((nbd, D_MODEL), f32),
                   jax.ShapeDtypeStruct(c0.shape, f32),
                   jax.ShapeDtypeStruct(n0.shape, f32),
                   jax.ShapeDtypeStruct((nbd, LANES), f32),
                   jax.ShapeDtypeStruct(conv0.shape, f32)),
        compiler_params=pltpu.CompilerParams(dimension_semantics=("arbitrary",), vmem_limit_bytes=VMEM_LIMIT),
        name="sample_tail",
    )(x_s, ctx, sza, m_in, szm, conv0, c0, n0, m0, *weights)


def _prep_weights(l, norm_g, w_in, q_norm_g, w_uq, kv_norm_g, w_uk, w_uv, g_qn, g_qr, g_kn, g_kr,
                  conv_w, conv_b, w_q_m, w_k_m, w_v_m, w_gate, b_i, b_f, mh_norm_g, skip, w_out):
    wi = w_in[l]
    o1 = Q_LORA + KV_LORA + QK_ROPE
    w1 = jnp.concatenate([wi[:, :o1], jnp.zeros((D_MODEL, LANES - QK_ROPE), f32)], axis=1)
    w2 = wi[:, o1:]
    wq = w_uq[l].reshape(Q_LORA, N_HEADS_A, QK_NOPE + QK_ROPE)
    wq_rope = jnp.concatenate([wq[:, :, QK_NOPE:], jnp.zeros((Q_LORA, N_HEADS_A, LANES - QK_ROPE), f32)], axis=2)
    wuq = jnp.concatenate([wq[:, :, :QK_NOPE].reshape(Q_LORA, -1), wq_rope.reshape(Q_LORA, -1)], axis=1)
    pad_rope = lambda g: jnp.concatenate([g, jnp.zeros((LANES - QK_ROPE,), f32)])[None, :]
    wuk = w_uk[l].reshape(KV_LORA, N_HEADS_A * QK_NOPE)
    wg = jnp.concatenate([w_gate[l], jnp.zeros((3 * D_MLS, LANES - 2 * N_HEADS_M), f32)], axis=1)
    gate_b = jnp.concatenate([b_i[l], b_f[l], jnp.zeros((LANES - 2 * N_HEADS_M,), f32)])[None, :]
    return {
        "norm_g": norm_g[l][None, :],
        "w1": w1.astype(bf16),
        "w2": w2.astype(bf16),
        "q_norm_g": q_norm_g[l][None, :],
        "w_uq": wuq.astype(bf16),
        "kv_norm_g": kv_norm_g[l][None, :],
        "w_uk": wuk.astype(bf16),
        "w_uk_t": wuk.T.astype(bf16),
        "w_uv": jnp.transpose(w_uv[l], (1, 0, 2)).astype(bf16),
        "g_qn": g_qn[l][None, :],
        "g_qr": pad_rope(g_qr[l]),
        "g_kn": g_kn[l][None, :],
        "g_kr": pad_rope(g_kr[l]),
        "conv_w": conv_w[l],
        "conv_b": conv_b[l][None, :],
        "w_qk_m": jnp.concatenate([w_q_m[l], w_k_m[l]], axis=2).astype(bf16),
        "w_v_m": w_v_m[l].astype(bf16),
        "w_gate": wg.astype(bf16),
        "gate_b": gate_b,
        "mh_norm_g": mh_norm_g[l][None, :],
        "skip": skip[l][None, :],
        "w_out_a": w_out[l][:D_ATT].astype(bf16),
        "w_out_m": w_out[l][D_ATT:].astype(bf16),
    }


def _pick(n, candidates):
    for c in candidates:
        if n % c == 0:
            return c
    raise ValueError(f"no tile size for extent {n}")


def kernel(x_prompt, x_sample, cache_ckv, cache_krope, state_C, state_n, state_m, state_conv, page_table,
           norm_g, w_in, q_norm_g, w_uq, kv_norm_g, w_uk, w_uv, g_qn, g_qr, g_kn, g_kr,
           conv_w, conv_b, w_q_m, w_k_m, w_v_m, w_gate, b_i, b_f, mh_norm_g, skip, w_out):
    nb, s, _ = x_prompt.shape
    nbd, sd, _ = x_sample.shape
    depth = norm_g.shape[0]
    assert sd == 1, "sample path handles one new token per sequence"
    npages = page_table.shape[1]
    past_len = npages * PAGE_SIZE

    cos_p, sin_p = _rope_tables(jnp.arange(s))
    cos_s, sin_s = _rope_tables(jnp.full((nbd,), past_len, jnp.int32))

    tm_p = _pick(s, (512, 256, 128))
    tk_att = _pick(s, (256, 128))
    tq_att = _pick(tm_p, (2 * tk_att, tk_att))
    ts_m = _pick(s, (512, 256, 128))
    tm_o = _pick(nb * s, (512, 256, 128))
    assert npages % 2 == 0, "sample attention double-buffers an even number of page blocks"
    pb = _pick(npages // 2, (32, 16, 8, 4, 2, 1))
    sub = _pick(pb * PAGE_SIZE, (2048, 1024, 512, 256, 128))
    bb = _pick(nbd, (16,))

    cache_krope_t = jnp.swapaxes(cache_krope, 2, 3)

    yp = x_prompt
    ys = x_sample.reshape(1, nbd, D_MODEL)
    outs_p, outs_s = [], []
    for l in range(depth):
        wp = _prep_weights(l, norm_g, w_in, q_norm_g, w_uq, kv_norm_g, w_uk, w_uv, g_qn, g_qr, g_kn, g_kr,
                           conv_w, conv_b, w_q_m, w_k_m, w_v_m, w_gate, b_i, b_f, mh_norm_g, skip, w_out)
        qt, k, ckv, ct, kr, sza, m_in, szm = _proj(yp, cos_p, sin_p, wp, tm_p, tq_att, tk_att)
        mixa = _attn_prompt(qt, k, ct, sza, wp["w_uv"], tq_att, tk_att)
        mixm, cn, mm, halo = _mlstm_prompt(m_in, szm, wp, ts_m, LANES)
        yp = _out_proj(yp.reshape(nb * s, D_MODEL), mixa.reshape(nb * s, D_ATT), mixm.reshape(nb * s, D_MLS),
                       wp, tm_o).reshape(nb, s, D_MODEL)
        outs_p.append((ckv, kr, cn[..., :DH_M], cn[..., DH_M], mm[:, :N_HEADS_M, 0],
                       halo[:, 8 - (CONV_W - 1):, :]))
        qt_s, k_s, ckv_s, _, kr_s, sza_s, m_in_s, szm_s = _proj(ys, cos_s, sin_s, wp, nbd, nbd, nbd)
        ctx = _attn_sample(page_table, jnp.transpose(qt_s[0, :, 0], (2, 0, 1)), jnp.transpose(k_s[0], (1, 0, 2)),
                           ckv_s.reshape(nbd, 1, KV_LORA), wp, cache_ckv, cache_krope_t, l, pb, sub)
        m0 = jnp.concatenate([state_m[l], jnp.zeros((nbd, LANES - N_HEADS_M), f32)], axis=1)
        to_lead = lambda a: jnp.transpose(a, (1, 0, 2))
        y_s, c1, n1, m1, conv1 = _sample_tail(ys[0], to_lead(ctx), sza_s[0], m_in_s[0], szm_s[0],
                                              to_lead(state_conv[l]), state_C[l], to_lead(state_n[l]), m0, wp, bb)
        ys = y_s.reshape(1, nbd, D_MODEL)
        outs_s.append((ckv_s.reshape(nbd, 1, KV_LORA), kr_s.reshape(nbd, 1, QK_ROPE), c1, to_lead(n1),
                       m1[:, :N_HEADS_M], to_lead(conv1)))
    stk = lambda outs, i: jnp.stack([o[i] for o in outs], axis=0)
    return (yp, ys.reshape(nbd, 1, D_MODEL),
            stk(outs_p, 0), stk(outs_p, 1), stk(outs_p, 2), stk(outs_p, 3), stk(outs_p, 4), stk(outs_p, 5),
            stk(outs_s, 0), stk(outs_s, 1), stk(outs_s, 2), stk(outs_s, 3), stk(outs_s, 4), stk(outs_s, 5))
```

```python
import functools

import jax
import jax.numpy as jnp
from jax import lax
from jax.experimental import pallas as pl
from jax.experimental.pallas import tpu as pltpu

f32 = jnp.float32
bf16 = jnp.bfloat16

D_MODEL = 1024
D_ATT = 512
D_MLS = 512
N_HEADS_A = 4
QK_NOPE = 128
QK_ROPE = 64
V_HEAD = 128
Q_LORA = 384
KV_LORA = 256
ROPE_BASE = 10000.0
N_HEADS_M = 4
DH_M = 128
CONV_W = 4
PAGE_SIZE = 128
EPS = 1e-6
LN_EPS = 1e-5
ATT_SCALE = (QK_NOPE + QK_ROPE) ** -0.5
LOG2E = 1.4426950408889634
Q_SCALE = ATT_SCALE * LOG2E
K_SCALE_M = DH_M ** -0.5
SEQ_LAG = 2
AHEAD = 4
PROJ_ROWS = 128

LANES = 128
QK_PAD = 256
VMEM_LIMIT = 56 * 1024 * 1024

NT_DIMS = (((1,), (1,)), ((), ()))


def _rms(x, n):
    ms = jnp.sum(x * x, axis=-1, keepdims=True) * (1.0 / n)
    return x * lax.rsqrt(ms + EPS)


def _silu(x):
    return x * jax.nn.sigmoid(x)


def _dot(a, b):
    return jnp.dot(a, b, preferred_element_type=f32)


def _dot_nt(a, b):
    return lax.dot_general(a, b, NT_DIMS, preferred_element_type=f32)


def _rep_lanes(x, n):
    return x if n == 1 else jnp.concatenate([x] * n, axis=1)


def _rope_table_kernel(ang_ref, cos_ref, sin_ref):
    ang = ang_ref[...]
    lane = lax.broadcasted_iota(jnp.int32, ang.shape, 1)
    cos_ref[...] = jnp.cos(ang)
    s = jnp.sin(ang)
    sin_ref[...] = jnp.where((lane % QK_ROPE) < QK_ROPE // 2, -s, s)


def _rope_tables(pos):
    half = QK_ROPE // 2
    inv = 1.0 / (ROPE_BASE ** (jnp.arange(0, QK_ROPE, 2, dtype=f32) / QK_ROPE))
    ang = pos.astype(f32)[:, None] * inv[None, :]
    ang = jnp.tile(ang, (1, LANES // half))
    s = ang.shape[0]
    return pl.pallas_call(
        _rope_table_kernel,
        out_shape=(jax.ShapeDtypeStruct((s, LANES), f32), jax.ShapeDtypeStruct((s, LANES), f32)),
        name="rope_tables",
    )(ang)


def _proj_kernel(x_ref, cos_ref, sin_ref, ng_ref, w1_ref, w2_ref, qg_ref, wuq_ref, kvg_ref, wuk_ref,
                 gqn_ref, gqr_ref, gkn_ref, gkr_ref,
                 qt_ref, k_ref, ckv_ref, ct_ref, kr_ref, sza_ref, min_ref, szm_ref):
    tm = x_ref.shape[1]
    rows = min(tm, PROJ_ROWS)
    tq = qt_ref.shape[4]
    tk = ct_ref.shape[3]
    lane = lax.broadcasted_iota(jnp.int32, (rows, LANES), 1)
    first_half = (lane % QK_ROPE) < QK_ROPE // 2

    for r0 in range(0, tm, rows):
        rs = slice(r0, r0 + rows)
        x = x_ref[0, rs, :]
        h = (_rms(x, D_MODEL) * ng_ref[...]).astype(bf16)
        p1 = _dot(h, w1_ref[...])
        p2 = _dot(h, w2_ref[...])
        cos = cos_ref[rs, :]
        sin = sin_ref[rs, :]

        def rope(xp):
            sw = jnp.where(first_half, pltpu.roll(xp, LANES - QK_ROPE // 2, 1), pltpu.roll(xp, QK_ROPE // 2, 1))
            return xp * cos + sw * sin

        ql = (_rms(p1[:, :Q_LORA], Q_LORA) * qg_ref[...]).astype(bf16)
        qf = _dot(ql, wuq_ref[...])
        for hh in range(N_HEADS_A):
            qn = _rms(qf[:, LANES * hh:LANES * (hh + 1)], QK_NOPE) * gqn_ref[...]
            o = N_HEADS_A * QK_NOPE + LANES * hh
            qr = rope(_rms(qf[:, o:o + LANES], QK_ROPE) * gqr_ref[...])
            qh = jnp.concatenate([qn, qr], axis=1) * Q_SCALE
            qt_ref[0, hh, r0 // tq, :, r0 % tq:r0 % tq + rows] = qh.T.astype(bf16)

        c = _rms(p1[:, Q_LORA:Q_LORA + KV_LORA], KV_LORA) * kvg_ref[...]
        ckv_ref[0, rs, :] = c
        cb = c.astype(bf16)
        ct_ref[0, r0 // tk, :, r0 % tk:r0 % tk + rows] = c.T.astype(bf16)
        kn = _dot(cb, wuk_ref[...])
        o = Q_LORA + KV_LORA
        krp = rope(_rms(p1[:, o:o + LANES], QK_ROPE) * gkr_ref[...])
        kr_ref[0, rs, :] = krp[:, :QK_ROPE]
        krb = krp.astype(bf16)
        for hh in range(N_HEADS_A):
            knh = _rms(kn[:, LANES * hh:LANES * (hh + 1)], QK_NOPE) * gkn_ref[...]
            k_ref[0, hh, rs, 0:LANES] = knh.astype(bf16)
            k_ref[0, hh, rs, LANES:QK_PAD] = krb

        sza_ref[0, rs, :] = _silu(p2[:, :D_ATT]).astype(bf16)
        min_ref[0, rs, :] = p2[:, D_ATT:D_ATT + D_MLS]
        szm_ref[0, rs, :] = _silu(p2[:, D_ATT + D_MLS:]).astype(bf16)


def _proj(x, cos, sin, wp, tm, tq, tk):
    nb, s, _ = x.shape
    grid = (nb, s // tm)
    full = lambda a: pl.BlockSpec(a.shape, lambda b, i: (0,) * a.ndim)
    tok = lambda w: pl.BlockSpec((1, tm, w), lambda b, i: (b, i, 0))
    head = pl.BlockSpec((1, N_HEADS_A, tm, QK_PAD), lambda b, i: (b, 0, i, 0))
    head_t = pl.BlockSpec((1, N_HEADS_A, tm // tq, QK_PAD, tq), lambda b, i: (b, 0, i, 0, 0))
    lat_t = pl.BlockSpec((1, tm // tk, KV_LORA, tk), lambda b, i: (b, i, 0, 0))
    tab = pl.BlockSpec((tm, LANES), lambda b, i: (i, 0))
    weights = (wp["norm_g"], wp["w1"], wp["w2"], wp["q_norm_g"], wp["w_uq"], wp["kv_norm_g"], wp["w_uk"],
               wp["g_qn"], wp["g_qr"], wp["g_kn"], wp["g_kr"])
    out_shape = (
        jax.ShapeDtypeStruct((nb, N_HEADS_A, s // tq, QK_PAD, tq), bf16),
        jax.ShapeDtypeStruct((nb, N_HEADS_A, s, QK_PAD), bf16),
        jax.ShapeDtypeStruct((nb, s, KV_LORA), f32),
        jax.ShapeDtypeStruct((nb, s // tk, KV_LORA, tk), bf16),
        jax.ShapeDtypeStruct((nb, s, QK_ROPE), f32),
        jax.ShapeDtypeStruct((nb, s, D_ATT), bf16),
        jax.ShapeDtypeStruct((nb, s, D_MLS), f32),
        jax.ShapeDtypeStruct((nb, s, D_MLS), bf16),
    )
    out_specs = (head_t, head, tok(KV_LORA), lat_t, tok(QK_ROPE), tok(D_ATT), tok(D_MLS), tok(D_MLS))
    return pl.pallas_call(
        _proj_kernel,
        grid=grid,
        in_specs=[tok(D_MODEL), tab, tab] + [full(w) for w in weights],
        out_specs=out_specs,
        out_shape=out_shape,
        compiler_params=pltpu.CompilerParams(dimension_semantics=("arbitrary", "arbitrary"),
                                             vmem_limit_bytes=VMEM_LIMIT),
        name="in_proj",
    )(x, cos, sin, *weights)


def _attn_kernel(qt_ref, k_ref, ct_ref, sza_ref, wuv_ref, o_ref, m_sc, l_sc, acc_sc, s_sc, *, tq, tk):
    i = pl.program_id(1)
    ratio = tq // tk
    kv_idx = lax.broadcasted_iota(jnp.int32, (tk, tq), 0)
    q_idx = lax.broadcasted_iota(jnp.int32, (tk, tq), 1)

    m_sc[...] = jnp.full(m_sc.shape, -jnp.inf, f32)
    l_sc[...] = jnp.zeros(l_sc.shape, f32)
    acc_sc[...] = jnp.zeros(acc_sc.shape, f32)

    def step(j, diag):
        start = pl.multiple_of(j * tk, tk)
        ctj = ct_ref[0, j]

        def scores(hh):
            s = _dot(k_ref[0, hh, pl.ds(start, tk), :], qt_ref[0, hh, 0])
            s_sc[hh] = s if diag is None else jnp.where(kv_idx + diag * tk <= q_idx, s, -jnp.inf)

        def update(hh):
            s = s_sc[hh]
            m_old = m_sc[hh]
            m_new = jnp.maximum(m_old, jnp.max(s, axis=0, keepdims=True))
            alpha = jnp.exp2(m_old - m_new)
            p = jnp.exp2(s - m_new[0:1, :])
            l_sc[hh] = alpha * l_sc[hh] + jnp.sum(p, axis=0, keepdims=True)
            acc_sc[hh] = alpha[0:1, :] * acc_sc[hh] + _dot(ctj, p.astype(bf16))
            m_sc[hh] = m_new

        for hh in range(min(AHEAD, N_HEADS_A)):
            scores(hh)
        for hh in range(N_HEADS_A):
            if hh + AHEAD < N_HEADS_A:
                scores(hh + AHEAD)
            update(hh)

    def body(j, carry):
        step(j, None)
        return carry

    lax.fori_loop(0, ratio * i, body, 0)
    for d in range(ratio):
        step(ratio * i + d, d)

    for hh in range(N_HEADS_A):
        inv = 1.0 / l_sc[hh]
        ctx = (acc_sc[hh] * inv[0:1, :]).T.astype(bf16)
        att = _dot(ctx, wuv_ref[hh])
        sl = slice(V_HEAD * hh, V_HEAD * (hh + 1))
        o_ref[0, :, sl] = (att * sza_ref[0, :, sl].astype(f32)).astype(bf16)


def _attn_prompt(qt, k, ct, sza, wuv, tq, tk):
    nb, _, s, _ = k.shape
    assert tq % tk == 0
    assert qt.shape == (nb, N_HEADS_A, s // tq, QK_PAD, tq) and ct.shape == (nb, s // tk, KV_LORA, tk)
    return pl.pallas_call(
        functools.partial(_attn_kernel, tq=tq, tk=tk),
        grid=(nb, s // tq),
        in_specs=[
            pl.BlockSpec((1, N_HEADS_A, 1, QK_PAD, tq), lambda b, i: (b, 0, i, 0, 0)),
            pl.BlockSpec((1, N_HEADS_A, s, QK_PAD), lambda b, i: (b, 0, 0, 0)),
            pl.BlockSpec((1, s // tk, KV_LORA, tk), lambda b, i: (b, 0, 0, 0)),
            pl.BlockSpec((1, tq, D_ATT), lambda b, i: (b, i, 0)),
            pl.BlockSpec(wuv.shape, lambda b, i: (0, 0, 0)),
        ],
        out_specs=pl.BlockSpec((1, tq, D_ATT), lambda b, i: (b, i, 0)),
        out_shape=jax.ShapeDtypeStruct((nb, s, D_ATT), bf16),
        scratch_shapes=[pltpu.VMEM((N_HEADS_A, 8, tq), f32),
                        pltpu.VMEM((N_HEADS_A, 8, tq), f32),
                        pltpu.VMEM((N_HEADS_A, KV_LORA, tq), f32),
                        pltpu.VMEM((N_HEADS_A, tk, tq), f32)],
        compiler_params=pltpu.CompilerParams(dimension_semantics=("arbitrary", "arbitrary"),
                                             vmem_limit_bytes=VMEM_LIMIT),
        name="attn_prompt",
    )(qt, k, ct, sza, wuv)


def _scan_lanes(x, op, fill, length):
    lane = lax.broadcasted_iota(jnp.int32, x.shape, 1) % length
    d = 1
    while d < length:
        x = op(x, jnp.where(lane >= d, pltpu.roll(x, d, 1), fill))
        d *= 2
    return x


def _mlstm_kernel(min_ref, szm_ref, cw_ref, cb_ref, wqk_ref, wv_ref, wg_ref, gb_ref, lng_ref, skip_ref, sel_ref,
                  mix_ref, cn_ref, m_ref, halo_ref,
                  xbuf, ca_sc, qkv_sc, ks_sc, cn_sc, m_sc, cols_sc, inter_sc, sqk_sc, *, ts, lc, ne):
    si = pl.program_id(1)
    ns = pl.num_programs(1)
    nc = ts // lc

    @pl.when(si == 0)
    def _():
        xbuf[:, 0:8, :] = jnp.zeros((ne, 8, D_MLS), f32)
        cn_sc[...] = jnp.zeros(cn_sc.shape, f32)
        m_sc[...] = jnp.zeros(m_sc.shape, f32)

    row = lax.broadcasted_iota(jnp.int32, (lc, lc), 0)
    col = lax.broadcasted_iota(jnp.int32, (lc, lc), 1)
    causal = col <= row
    ones_b = jnp.ones((lc, LANES), bf16)

    def projections(e):
        x = min_ref[e]
        xbuf[e, 8:8 + ts, :] = x
        y = cb_ref[...] + xbuf[e, 8:8 + ts, :] * cw_ref[3:4, :]
        for j in range(CONV_W - 1):
            y = y + xbuf[e, 5 + j:5 + j + ts, :] * cw_ref[j:j + 1, :]
        xbuf[e, 0:8, :] = xbuf[e, ts:ts + 8, :]
        ca = _silu(y)
        ca_sc[e] = ca
        cab = ca.astype(bf16)
        xb = x.astype(bf16)
        for hh in range(N_HEADS_M):
            sl = slice(DH_M * hh, DH_M * (hh + 1))
            qk = _dot(cab[:, sl], wqk_ref[hh])
            qkv_sc[e, :, sl] = qk[:, :DH_M].astype(bf16)
            kh = qk[:, DH_M:]
            qkv_sc[e, :, D_MLS + DH_M * hh:D_MLS + DH_M * (hh + 1)] = kh.astype(bf16)
            ks_sc[e, :, sl] = (kh * K_SCALE_M).astype(bf16)
            qkv_sc[e, :, 2 * D_MLS + DH_M * hh:2 * D_MLS + DH_M * (hh + 1)] = (
                _dot(xb[:, sl], wv_ref[hh]).astype(bf16))

    def gate_scans(e):
        g = _dot(qkv_sc[e], wg_ref[...]) + gb_ref[...]
        gt = g.T[0:8, :]
        head_row = lax.broadcasted_iota(jnp.int32, gt.shape, 0) < N_HEADS_M
        li_all = jnp.where(head_row, gt, 0.0)
        lf_all = jnp.where(head_row, jax.nn.log_sigmoid(pltpu.roll(gt, 4, 0)), 0.0)
        for c in range(nc):
            for hh in range(N_HEADS_M):
                cs = slice(lc * c, lc * (c + 1))
                sl = slice(DH_M * hh, DH_M * (hh + 1))
                sqk_sc[e, c, hh] = _dot_nt(qkv_sc[e, cs, sl], ks_sc[e, cs, sl])
        b_all = _scan_lanes(lf_all, jnp.add, 0.0, lc)
        u_all = li_all - b_all
        cm_all = _scan_lanes(u_all, jnp.maximum, -jnp.inf, lc)
        u_max = [jnp.max(u_all[:, lc * c:lc * (c + 1)], axis=1, keepdims=True) for c in range(nc)]
        lf_sum = [jnp.sum(lf_all[:, lc * c:lc * (c + 1)], axis=1, keepdims=True) for c in range(nc)]
        return b_all, u_all, cm_all, u_max, lf_sum

    def chunk_refs(e, c, hh):
        cs = slice(lc * c, lc * (c + 1))
        sl = slice(DH_M * hh, DH_M * (hh + 1))
        vc = qkv_sc[e, cs, 2 * D_MLS + DH_M * hh:2 * D_MLS + DH_M * (hh + 1)]
        v_aug = jnp.concatenate([vc, ones_b], axis=1)
        return cs, sl, qkv_sc[e, cs, sl], ks_sc[e, cs, sl], v_aug

    def state_pass(e, c, scans, m0):
        b_all, u_all, cm_all, u_max, lf_sum = scans
        cs = slice(lc * c, lc * (c + 1))
        u = u_all[:, cs]
        gmax = jnp.maximum(m0, cm_all[:, cs])
        mt = b_all[:, cs] + gmax
        g_last = jnp.maximum(m0, u_max[c])
        m_new = lf_sum[c] + g_last
        decay = jnp.exp(m0 - g_last)
        w_s = jnp.exp(u - g_last)
        r = jnp.concatenate([gmax, mt], axis=0)
        r_hi = r.astype(bf16).astype(f32)
        r_mid = (r - r_hi).astype(bf16).astype(f32)
        r_lo = r - r_hi - r_mid
        stack = jnp.concatenate([r_hi, r_mid, r_lo, jnp.zeros((lc - 48, lc), f32)], axis=0)
        cols_sc[e, c] = _dot(stack.T.astype(bf16), sel_ref[...])
        for hh in range(N_HEADS_M):
            _, _, qc, kc, v_aug = chunk_refs(e, c, hh)
            inter_sc[e, c, hh] = _dot(qc, cn_sc[e, hh].astype(bf16))
            ktw = (kc.astype(f32).T * w_s[hh:hh + 1, :]).astype(bf16)
            upd = _dot(ktw, v_aug)
            dec = jnp.broadcast_to(decay[hh:hh + 1, :], (DH_M, LANES))
            cn_sc[e, hh] = _rep_lanes(dec, 2) * cn_sc[e, hh] + upd
        return u, m_new

    def output_pass(e, c, u, m0):
        for hh in range(N_HEADS_M):
            cs, sl, _, _, v_aug = chunk_refs(e, c, hh)
            g_col = cols_sc[e, c, :, LANES * hh:LANES * (hh + 1)]
            mt_col = cols_sc[e, c, :, LANES * (N_HEADS_M + hh):LANES * (N_HEADS_M + hh + 1)]
            wm = jnp.where(causal, jnp.exp(u[hh:hh + 1, :] - g_col), 0.0)
            p = (sqk_sc[e, c, hh] * wm).astype(bf16)
            intra = _dot(p, v_aug)
            w_inter = jnp.exp(m0[hh:hh + 1, :] - g_col)
            tot = _rep_lanes(w_inter, 2) * inter_sc[e, c, hh] + intra
            num = tot[:, :DH_M]
            den = tot[:, DH_M:]
            hv = num / jnp.maximum(jnp.abs(den), jnp.exp(-mt_col))
            mu = jnp.mean(hv, axis=-1, keepdims=True)
            hc = hv - mu
            var = jnp.mean(hc * hc, axis=-1, keepdims=True)
            hn = hc * lax.rsqrt(var + LN_EPS) * lng_ref[:, sl]
            out = hn + skip_ref[:, sl] * ca_sc[e, cs, sl]
            mix_ref[e, cs, sl] = (out * szm_ref[e, cs, sl].astype(f32)).astype(bf16)

    seqs = range(ne)
    scans = []
    for e in seqs:
        projections(e)
        scans.append(gate_scans(e))
    m_cur = [m_sc[e] for e in seqs]
    staged = [[] for _ in seqs]
    for t in range(nc + 1 + SEQ_LAG * (ne - 1)):
        for e in seqs:
            c = t - SEQ_LAG * e
            if 0 <= c < nc:
                u_c, m_next = state_pass(e, c, scans[e], m_cur[e])
                staged[e].append((u_c, m_cur[e]))
                m_cur[e] = m_next
            if 1 <= c <= nc:
                output_pass(e, c - 1, *staged[e][c - 1])
    for e in seqs:
        m_sc[e] = m_cur[e]

    @pl.when(si == ns - 1)
    def _():
        cn_ref[...] = cn_sc[...]
        m_ref[...] = m_sc[...]
        halo_ref[...] = xbuf[:, 0:8, :]


def _column_selector():
    rows = lax.broadcasted_iota(jnp.int32, (LANES, 2 * N_HEADS_M * LANES), 0)
    block = lax.broadcasted_iota(jnp.int32, (LANES, 2 * N_HEADS_M * LANES), 1) // LANES
    src = 8 * (block // N_HEADS_M) + block % N_HEADS_M
    return ((rows < 48) & (rows % 16 == src)).astype(bf16)


def _mlstm_prompt(m_in, szm, wp, ts, lc):
    nb, s, _ = m_in.shape
    full = lambda a: pl.BlockSpec(a.shape, lambda b, i: (0,) * a.ndim)
    ne = 2 if nb % 2 == 0 else 1
    tok = pl.BlockSpec((ne, ts, D_MLS), lambda b, i: (b, i, 0))
    assert lc == LANES, "one chunk spans exactly one vreg of lanes"
    weights = (wp["conv_w"], wp["conv_b"], wp["w_qk_m"], wp["w_v_m"], wp["w_gate"], wp["gate_b"],
               wp["mh_norm_g"], wp["skip"], _column_selector())
    nc = ts // lc
    return pl.pallas_call(
        functools.partial(_mlstm_kernel, ts=ts, lc=lc, ne=ne),
        grid=(nb // ne, s // ts),
        in_specs=[tok, tok] + [full(w) for w in weights],
        out_specs=(tok,
                   pl.BlockSpec((ne, N_HEADS_M, DH_M, 2 * DH_M), lambda b, i: (b, 0, 0, 0)),
                   pl.BlockSpec((ne, 8, LANES), lambda b, i: (b, 0, 0)),
                   pl.BlockSpec((ne, 8, D_MLS), lambda b, i: (b, 0, 0))),
        out_shape=(jax.ShapeDtypeStruct((nb, s, D_MLS), bf16),
                   jax.ShapeDtypeStruct((nb, N_HEADS_M, DH_M, 2 * DH_M), f32),
                   jax.ShapeDtypeStruct((nb, 8, LANES), f32),
                   jax.ShapeDtypeStruct((nb, 8, D_MLS), f32)),
        scratch_shapes=[pltpu.VMEM((ne, ts + 8, D_MLS), f32),
                        pltpu.VMEM((ne, ts, D_MLS), f32),
                        pltpu.VMEM((ne, ts, 3 * D_MLS), bf16),
                        pltpu.VMEM((ne, ts, D_MLS), bf16),
                        pltpu.VMEM((ne, N_HEADS_M, DH_M, 2 * DH_M), f32),
                        pltpu.VMEM((ne, 8, LANES), f32),
                        pltpu.VMEM((ne, nc, lc, 2 * N_HEADS_M * LANES), f32),
                        pltpu.VMEM((ne, nc, N_HEADS_M, lc, 2 * DH_M), f32),
                        pltpu.VMEM((ne, nc, N_HEADS_M, lc, lc), f32)],
        compiler_params=pltpu.CompilerParams(dimension_semantics=("arbitrary", "arbitrary"),
                                             vmem_limit_bytes=VMEM_LIMIT),
        name="mlstm_prompt",
    )(m_in, szm, *weights)


def _out_kernel(x_ref, ma_ref, mm_ref, wa_ref, wm_ref, y_ref):
    y_ref[...] = x_ref[...] + _dot(ma_ref[...], wa_ref[...]) + _dot(mm_ref[...], wm_ref[...])


def _out_proj(x2, mixa, mixm, wp, tm):
    n = x2.shape[0]
    full = lambda a: pl.BlockSpec(a.shape, lambda i: (0,) * a.ndim)
    return pl.pallas_call(
        _out_kernel,
        grid=(n // tm,),
        in_specs=[pl.BlockSpec((tm, D_MODEL), lambda i: (i, 0)),
                  pl.BlockSpec((tm, D_ATT), lambda i: (i, 0)),
                  pl.BlockSpec((tm, D_MLS), lambda i: (i, 0)),
                  full(wp["w_out_a"]), full(wp["w_out_m"])],
        out_specs=pl.BlockSpec((tm, D_MODEL), lambda i: (i, 0)),
        out_shape=jax.ShapeDtypeStruct((n, D_MODEL), f32),
        compiler_params=pltpu.CompilerParams(dimension_semantics=("arbitrary",), vmem_limit_bytes=VMEM_LIMIT),
        name="out_proj",
    )(x2, mixa, mixm, wp["w_out_a"], wp["w_out_m"])


def _sattn_kernel(pt_ref, q_ref, knew_ref, cnew_ref, wukt_ref, gkn_ref, ckv_hbm, krt_hbm, ctx_ref,
                  cbuf0, cbuf1, kbuf0, kbuf1, sem, cb_sc, knt_sc, lhs_sc, *, layer, pb, sub):
    b = pl.program_id(0)
    nbatch = pl.num_programs(0)
    npages = pt_ref.shape[1]
    nblk = npages // pb
    tb = pb * PAGE_SIZE
    nsub = tb // sub
    sub_pages = sub // PAGE_SIZE
    cbufs = (cbuf0, cbuf1)
    kbufs = (kbuf0, kbuf1)

    def page_copies(bb, blk, slot):
        latent, ropek = [], []
        for p in range(pb):
            page = pt_ref[bb, blk * pb + p]
            rows = pl.ds(p * PAGE_SIZE, PAGE_SIZE)
            latent.append(pltpu.make_async_copy(ckv_hbm.at[layer, page], cbufs[slot].at[rows], sem.at[slot, 0]))
            ropek.append(pltpu.make_async_copy(krt_hbm.at[layer, page], kbufs[slot].at[p], sem.at[slot, 1]))
        return latent + ropek

    def start_block(bb, blk, slot):
        for cp in page_copies(bb, blk, slot):
            cp.start()

    def wait_block(bb, blk, slot):
        for cp in page_copies(bb, blk, slot):
            cp.wait()

    @pl.when(b == 0)
    def _():
        start_block(0, 0, 0)

    q = q_ref[0]
    qg = q[:, :QK_NOPE].astype(f32) * gkn_ref[...]
    rowq = lax.broadcasted_iota(jnp.int32, (16, QK_NOPE), 0)
    qabs = jnp.zeros((16, KV_LORA), f32)
    for hh in range(N_HEADS_A):
        lhs = jnp.where(rowq == hh, jnp.broadcast_to(qg[hh:hh + 1, :], (16, QK_NOPE)), 0.0).astype(bf16)
        qabs = qabs + _dot(lhs, wukt_ref[QK_NOPE * hh:QK_NOPE * (hh + 1), :])
    nk = N_HEADS_A * QK_NOPE
    lhs_sc[0:nk, :] = wukt_ref[...]
    lhs_sc[nk:nk + 16, :] = qabs.astype(bf16)
    qr16 = jnp.concatenate([q[:, QK_NOPE:QK_NOPE + QK_ROPE],
                            jnp.zeros((16 - N_HEADS_A, QK_ROPE), bf16)], axis=0)

    row16 = lax.broadcasted_iota(jnp.int32, (16, sub), 0)

    def matmuls(slot, si):
        cb = cbufs[slot][sub * si:sub * (si + 1), :].astype(bf16)
        cb_sc[si % 2] = cb
        knt_sc[si % 2] = _dot_nt(lhs_sc[...], cb)
        srope = jnp.concatenate(
            [_dot(qr16, kbufs[slot][sub_pages * si + p].astype(bf16)) for p in range(sub_pages)],
            axis=1)
        return srope

    def softmax_update(si, srope, carry):
        num = knt_sc[si % 2, nk:nk + 16, :]
        m, l, acc = carry
        rnorm = jnp.zeros((16, sub), f32)
        for hh in range(N_HEADS_A):
            kh = knt_sc[si % 2, QK_NOPE * hh:QK_NOPE * (hh + 1), :]
            ss = jnp.sum(kh * kh, axis=0, keepdims=True)
            rnorm = jnp.where(row16 == hh, lax.rsqrt(ss * (1.0 / QK_NOPE) + EPS), rnorm)
        s = num * rnorm + srope
        m_new = jnp.maximum(m, jnp.max(s, axis=1, keepdims=True))
        alpha = jnp.exp2(m - m_new)
        p = jnp.exp2(s - _rep_lanes(m_new, sub // LANES))
        l = alpha * l + jnp.sum(p, axis=1, keepdims=True)
        acc = _rep_lanes(alpha, KV_LORA // LANES) * acc + _dot(p.astype(bf16), cb_sc[si % 2])
        return m_new, l, acc

    def compute(slot, carry):
        staged = matmuls(slot, 0)
        for si in range(nsub):
            nxt = matmuls(slot, si + 1) if si + 1 < nsub else None
            carry = softmax_update(si, staged, carry)
            staged = nxt
        return carry

    def pair(i, carry):
        blk = 2 * i
        wait_block(b, blk, 0)
        start_block(b, blk + 1, 1)
        carry = compute(0, carry)
        wait_block(b, blk + 1, 1)
        wraps = blk + 2 == nblk
        start_block(jnp.where(wraps, jnp.minimum(b + 1, nbatch - 1), b), jnp.where(wraps, 0, blk + 2), 0)
        return compute(1, carry)

    init = (jnp.full((16, LANES), -jnp.inf, f32), jnp.zeros((16, LANES), f32), jnp.zeros((16, KV_LORA), f32))
    m_old, l_old, acc_old = lax.fori_loop(0, nblk // 2, pair, init)

    @pl.when(b == nbatch - 1)
    def _():
        wait_block(b, 0, 0)

    kn = knew_ref[0].astype(f32)
    s_new = jnp.sum(q.astype(f32) * kn, axis=1, keepdims=True)
    s_new = jnp.concatenate([jnp.broadcast_to(s_new, (N_HEADS_A, LANES)),
                             jnp.zeros((16 - N_HEADS_A, LANES), f32)], axis=0)
    m_new = jnp.maximum(m_old, s_new)
    alpha = jnp.exp2(m_old - m_new)
    p_new = jnp.exp2(s_new - m_new)
    l = alpha * l_old + p_new
    acc = _rep_lanes(alpha, 2) * acc_old + _rep_lanes(p_new, 2) * cnew_ref[0]
    ctx = acc * _rep_lanes(1.0 / l, 2)
    ctx_ref[0] = ctx[0:N_HEADS_A, :]


def _attn_sample(page_table, q_s, k_s, c_s, wp, cache_ckv, cache_krope_t, layer, pb, sub):
    nbd = q_s.shape[0]
    tb = pb * PAGE_SIZE
    full = lambda a: pl.BlockSpec(a.shape, lambda b, pt: (0,) * a.ndim)
    grid_spec = pltpu.PrefetchScalarGridSpec(
        num_scalar_prefetch=1,
        grid=(nbd,),
        in_specs=[pl.BlockSpec((1, N_HEADS_A, QK_PAD), lambda b, pt: (b, 0, 0)),
                  pl.BlockSpec((1, N_HEADS_A, QK_PAD), lambda b, pt: (b, 0, 0)),
                  pl.BlockSpec((1, 1, KV_LORA), lambda b, pt: (b, 0, 0)),
                  full(wp["w_uk_t"]), full(wp["g_kn"]),
                  pl.BlockSpec(memory_space=pl.ANY),
                  pl.BlockSpec(memory_space=pl.ANY)],
        out_specs=pl.BlockSpec((1, N_HEADS_A, KV_LORA), lambda b, pt: (b, 0, 0)),
        scratch_shapes=[pltpu.VMEM((tb, KV_LORA), f32),
                        pltpu.VMEM((tb, KV_LORA), f32),
                        pltpu.VMEM((pb, QK_ROPE, PAGE_SIZE), f32),
                        pltpu.VMEM((pb, QK_ROPE, PAGE_SIZE), f32),
                        pltpu.SemaphoreType.DMA((2, 2)),
                        pltpu.VMEM((2, sub, KV_LORA), bf16),
                        pltpu.VMEM((2, N_HEADS_A * QK_NOPE + 16, sub), f32),
                        pltpu.VMEM((N_HEADS_A * QK_NOPE + 16, KV_LORA), bf16)],
    )
    return pl.pallas_call(
        functools.partial(_sattn_kernel, layer=layer, pb=pb, sub=sub),
        grid_spec=grid_spec,
        out_shape=jax.ShapeDtypeStruct((nbd, N_HEADS_A, KV_LORA), f32),
        compiler_params=pltpu.CompilerParams(dimension_semantics=("arbitrary",), vmem_limit_bytes=VMEM_LIMIT),
        name="attn_sample",
    )(page_table, q_s, k_s, c_s, wp["w_uk_t"], wp["g_kn"], cache_ckv, cache_krope_t)


def _stail_kernel(x_ref, ctx_ref, sza_ref, min_ref, szm_ref, conv_ref, c0_ref, n0_ref, m0_ref,
                  wuv_ref, cw_ref, cb_ref, wqk_ref, wv_ref, wg_ref, gb_ref, lng_ref, skip_ref, woa_ref, wom_ref,
                  y_ref, c1_ref, n1_ref, m1_ref, conv1_ref, *, bb):
    mixa = []
    for hh in range(N_HEADS_A):
        att = _dot(ctx_ref[hh].astype(bf16), wuv_ref[hh])
        sl = slice(V_HEAD * hh, V_HEAD * (hh + 1))
        mixa.append((att * sza_ref[:, sl].astype(f32)).astype(bf16))
    mixa = jnp.concatenate(mixa, axis=1)

    x = min_ref[...]
    y = cb_ref[...] + x * cw_ref[3:4, :]
    for j in range(CONV_W - 1):
        y = y + conv_ref[j] * cw_ref[j:j + 1, :]
    for j in range(CONV_W - 2):
        conv1_ref[j] = conv_ref[j + 1]
    conv1_ref[CONV_W - 2] = x
    ca = _silu(y)
    cab = ca.astype(bf16)
    xb = x.astype(bf16)
    qs, ks, vs = [], [], []
    for hh in range(N_HEADS_M):
        sl = slice(DH_M * hh, DH_M * (hh + 1))
        qk = _dot(cab[:, sl], wqk_ref[hh])
        qs.append(qk[:, :DH_M])
        ks.append(qk[:, DH_M:])
        vs.append(_dot(xb[:, sl], wv_ref[hh]))
    qkv = jnp.concatenate(qs + ks + vs, axis=1).astype(bf16)
    g = _dot(qkv, wg_ref[...]) + gb_ref[...]
    li = g
    lf = jax.nn.log_sigmoid(pltpu.roll(g, LANES - N_HEADS_M, 1))
    m0 = m0_ref[...]
    a = lf + m0
    mt = jnp.maximum(a, li)
    w_inter = jnp.exp(a - mt)
    w_new = jnp.exp(li - mt)
    emt = jnp.exp(-mt)
    m1_ref[...] = mt

    rowi = lax.broadcasted_iota(jnp.int32, (bb, DH_M), 0)
    rowp = lax.broadcasted_iota(jnp.int32, (LANES, DH_M), 0)
    zpad = jnp.zeros((LANES - bb, DH_M), f32)
    mixm = []
    for hh in range(N_HEADS_M):
        sl = slice(DH_M * hh, DH_M * (hh + 1))
        col = lambda z: jnp.broadcast_to(z[:, hh:hh + 1], (bb, DH_M))
        wi, wn, em = col(w_inter), col(w_new), col(emt)
        qh = qs[hh]
        kh = ks[hh] * K_SCALE_M
        vh = vs[hh]
        qb_ = qh.astype(bf16)
        kw = (kh * wn)
        kwt = jnp.concatenate([kw, zpad], axis=0).T.astype(bf16)
        vb_ = vh.astype(bf16)
        vpad = jnp.concatenate([vh, zpad], axis=0)
        inter = jnp.zeros((bb, DH_M), f32)
        for r in range(bb):
            c0 = c0_ref[r, hh]
            inter = inter + jnp.where(rowi == r, _dot(qb_, c0.astype(bf16)), 0.0)
            upd = _dot(kwt, jnp.where(rowp == r, vpad, 0.0).astype(bf16))
            dec = jnp.broadcast_to(wi[r:r + 1, :], (DH_M, DH_M))
            c1_ref[r, hh] = dec * c0 + upd
        n0 = n0_ref[hh]
        n1_ref[hh] = wi * n0 + kw
        qk_dot = jnp.sum(qb_.astype(f32) * kh.astype(bf16).astype(f32), axis=1, keepdims=True) * wn
        num = wi * inter + qk_dot * vb_.astype(f32)
        den = wi * jnp.sum(qb_.astype(f32) * n0.astype(bf16).astype(f32), axis=1, keepdims=True) + qk_dot
        hv = num / jnp.maximum(jnp.abs(den), em)
        mu = jnp.mean(hv, axis=-1, keepdims=True)
        hc = hv - mu
        var = jnp.mean(hc * hc, axis=-1, keepdims=True)
        hn = hc * lax.rsqrt(var + LN_EPS) * lng_ref[:, sl]
        out = hn + skip_ref[:, sl] * ca[:, sl]
        mixm.append((out * szm_ref[:, sl].astype(f32)).astype(bf16))
    mixm = jnp.concatenate(mixm, axis=1)
    y_ref[...] = x_ref[...] + _dot(mixa, woa_ref[...]) + _dot(mixm, wom_ref[...])


def _sample_tail(x_s, ctx, sza, m_in, szm, conv0, c0, n0, m0, wp, bb):
    nbd = x_s.shape[0]
    full = lambda a: pl.BlockSpec(a.shape, lambda i: (0,) * a.ndim)
    row = lambda w: pl.BlockSpec((bb, w), lambda i: (i, 0))
    weights = (wp["w_uv"], wp["conv_w"], wp["conv_b"], wp["w_qk_m"], wp["w_v_m"], wp["w_gate"], wp["gate_b"],
               wp["mh_norm_g"], wp["skip"], wp["w_out_a"], wp["w_out_m"])
    conv_spec = pl.BlockSpec((CONV_W - 1, bb, D_MLS), lambda i: (0, i, 0))
    c_spec = pl.BlockSpec((bb, N_HEADS_M, DH_M, DH_M), lambda i: (i, 0, 0, 0))
    n_spec = pl.BlockSpec((N_HEADS_M, bb, DH_M), lambda i: (0, i, 0))
    return pl.pallas_call(
        functools.partial(_stail_kernel, bb=bb),
        grid=(nbd // bb,),
        in_specs=[row(D_MODEL),
                  pl.BlockSpec((N_HEADS_A, bb, KV_LORA), lambda i: (0, i, 0)),
                  row(D_ATT), row(D_MLS), row(D_MLS), conv_spec, c_spec, n_spec, row(LANES)]
                 + [full(w) for w in weights],
        out_specs=(row(D_MODEL), c_spec, n_spec, row(LANES), conv_spec),
        out_shape=(jax.ShapeDtypeStruct((nbd, D_MODEL), f32),
                   jax.ShapeDtypeStruct(c0.shape, f32),
                   jax.ShapeDtypeStruct(n0.shape, f32),
                   jax.ShapeDtypeStruct((nbd, LANES), f32),
                   jax.ShapeDtypeStruct(conv0.shape, f32)),
        compiler_params=pltpu.CompilerParams(dimension_semantics=("arbitrary",), vmem_limit_bytes=VMEM_LIMIT),
        name="sample_tail",
    )(x_s, ctx, sza, m_in, szm, conv0, c0, n0, m0, *weights)


def _prep_weights(l, norm_g, w_in, q_norm_g, w_uq, kv_norm_g, w_uk, w_uv, g_qn, g_qr, g_kn, g_kr,
                  conv_w, conv_b, w_q_m, w_k_m, w_v_m, w_gate, b_i, b_f, mh_norm_g, skip, w_out):
    wi = w_in[l]
    o1 = Q_LORA + KV_LORA + QK_ROPE
    w1 = jnp.concatenate([wi[:, :o1], jnp.zeros((D_MODEL, LANES - QK_ROPE), f32)], axis=1)
    w2 = wi[:, o1:]
    wq = w_uq[l].reshape(Q_LORA, N_HEADS_A, QK_NOPE + QK_ROPE)
    wq_rope = jnp.concatenate([wq[:, :, QK_NOPE:], jnp.zeros((Q_LORA, N_HEADS_A, LANES - QK_ROPE), f32)], axis=2)
    wuq = jnp.concatenate([wq[:, :, :QK_NOPE].reshape(Q_LORA, -1), wq_rope.reshape(Q_LORA, -1)], axis=1)
    pad_rope = lambda g: jnp.concatenate([g, jnp.zeros((LANES - QK_ROPE,), f32)])[None, :]
    wuk = w_uk[l].reshape(KV_LORA, N_HEADS_A * QK_NOPE)
    wg = jnp.concatenate([w_gate[l], jnp.zeros((3 * D_MLS, LANES - 2 * N_HEADS_M), f32)], axis=1)
    gate_b = jnp.concatenate([b_i[l], b_f[l], jnp.zeros((LANES - 2 * N_HEADS_M,), f32)])[None, :]
    return {
        "norm_g": norm_g[l][None, :],
        "w1": w1.astype(bf16),
        "w2": w2.astype(bf16),
        "q_norm_g": q_norm_g[l][None, :],
        "w_uq": wuq.astype(bf16),
        "kv_norm_g": kv_norm_g[l][None, :],
        "w_uk": wuk.astype(bf16),
        "w_uk_t": wuk.T.astype(bf16),
        "w_uv": jnp.transpose(w_uv[l], (1, 0, 2)).astype(bf16),
        "g_qn": g_qn[l][None, :],
        "g_qr": pad_rope(g_qr[l]),
        "g_kn": g_kn[l][None, :],
        "g_kr": pad_rope(g_kr[l]),
        "conv_w": conv_w[l],
        "conv_b": conv_b[l][None, :],
        "w_qk_m": jnp.concatenate([w_q_m[l], w_k_m[l]], axis=2).astype(bf16),
        "w_v_m": w_v_m[l].astype(bf16),
        "w_gate": wg.astype(bf16),
        "gate_b": gate_b,
        "mh_norm_g": mh_norm_g[l][None, :],
        "skip": skip[l][None, :],
        "w_out_a": w_out[l][:D_ATT].astype(bf16),
        "w_out_m": w_out[l][D_ATT:].astype(bf16),
    }


def _pick(n, candidates):
    for c in candidates:
        if n % c == 0:
            return c
    raise ValueError(f"no tile size for extent {n}")


def kernel(x_prompt, x_sample, cache_ckv, cache_krope, state_C, state_n, state_m, state_conv, page_table,
           norm_g, w_in, q_norm_g, w_uq, kv_norm_g, w_uk, w_uv, g_qn, g_qr, g_kn, g_kr,
           conv_w, conv_b, w_q_m, w_k_m, w_v_m, w_gate, b_i, b_f, mh_norm_g, skip, w_out):
    nb, s, _ = x_prompt.shape
    nbd, sd, _ = x_sample.shape
    depth = norm_g.shape[0]
    assert sd == 1, "sample path handles one new token per sequence"
    npages = page_table.shape[1]
    past_len = npages * PAGE_SIZE

    cos_p, sin_p = _rope_tables(jnp.arange(s))
    cos_s, sin_s = _rope_tables(jnp.full((nbd,), past_len, jnp.int32))

    tm_p = _pick(s, (512, 256, 128))
    tk_att = _pick(s, (256, 128))
    tq_att = _pick(tm_p, (2 * tk_att, tk_att))
    ts_m = _pick(s, (512, 256, 128))
    tm_o = _pick(nb * s, (512, 256, 128))
    assert npages % 2 == 0, "sample attention double-buffers an even number of page blocks"
    pb = _pick(npages // 2, (32, 16, 8, 4, 2, 1))
    sub = _pick(pb * PAGE_SIZE, (2048, 1024, 512, 256, 128))
    bb = _pick(nbd, (16,))

    cache_krope_t = jnp.swapaxes(cache_krope, 2, 3)

    yp = x_prompt
    ys = x_sample.reshape(1, nbd, D_MODEL)
    outs_p, outs_s = [], []
    for l in range(depth):
        wp = _prep_weights(l, norm_g, w_in, q_norm_g, w_uq, kv_norm_g, w_uk, w_uv, g_qn, g_qr, g_kn, g_kr,
                           conv_w, conv_b, w_q_m, w_k_m, w_v_m, w_gate, b_i, b_f, mh_norm_g, skip, w_out)
        qt, k, ckv, ct, kr, sza, m_in, szm = _proj(yp, cos_p, sin_p, wp, tm_p, tq_att, tk_att)
        mixa = _attn_prompt(qt, k, ct, sza, wp["w_uv"], tq_att, tk_att)
        mixm, cn, mm, halo = _mlstm_prompt(m_in, szm, wp, ts_m, LANES)
        yp = _out_proj(yp.reshape(nb * s, D_MODEL), mixa.reshape(nb * s, D_ATT), mixm.reshape(nb * s, D_MLS),
                       wp, tm_o).reshape(nb, s, D_MODEL)
        outs_p.append((ckv, kr, cn[..., :DH_M], cn[..., DH_M], mm[:, :N_HEADS_M, 0],
                       halo[:, 8 - (CONV_W - 1):, :]))
        qt_s, k_s, ckv_s, _, kr_s, sza_s, m_in_s, szm_s = _proj(ys, cos_s, sin_s, wp, nbd, nbd, nbd)
        ctx = _attn_sample(page_table, jnp.transpose(qt_s[0, :, 0], (2, 0, 1)), jnp.transpose(k_s[0], (1, 0, 2)),
                           ckv_s.reshape(nbd, 1, KV_LORA), wp, cache_ckv, cache_krope_t, l, pb, sub)
        m0 = jnp.concatenate([state_m[l], jnp.zeros((nbd, LANES - N_HEADS_M), f32)], axis=1)
        to_lead = lambda a: jnp.transpose(a, (1, 0, 2))
        y_s, c1, n1, m1, conv1 = _sample_tail(ys[0], to_lead(ctx), sza_s[0], m_in_s[0], szm_s[0],
                                              to_lead(state_conv[l]), state_C[l], to_lead(state_n[l]), m0, wp, bb)
        ys = y_s.reshape(1, nbd, D_MODEL)
        outs_s.append((ckv_s.reshape(nbd, 1, KV_LORA), kr_s.reshape(nbd, 1, QK_ROPE), c1, to_lead(n1),
                       m1[:, :N_HEADS_M], to_lead(conv1)))
    stk = lambda outs, i: jnp.stack([o[i] for o in outs], axis=0)
    return (yp, ys.reshape(nbd, 1, D_MODEL),
            stk(outs_p, 0), stk(outs_p, 1), stk(outs_p, 2), stk(outs_p, 3), stk(outs_p, 4), stk(outs_p, 5),
            stk(outs_s, 0), stk(outs_s, 1), stk(outs_s, 2), stk(outs_s, 3), stk(outs_s, 4), stk(outs_s, 5))
```

```python
import functools

import jax
import jax.numpy as jnp
from jax import lax
from jax.experimental import pallas as pl
from jax.experimental.pallas import tpu as pltpu

f32 = jnp.float32
bf16 = jnp.bfloat16

D_MODEL = 1024
D_ATT = 512
D_MLS = 512
N_HEADS_A = 4
QK_NOPE = 128
QK_ROPE = 64
V_HEAD = 128
Q_LORA = 384
KV_LORA = 256
ROPE_BASE = 10000.0
N_HEADS_M = 4
DH_M = 128
CONV_W = 4
PAGE_SIZE = 128
EPS = 1e-6
LN_EPS = 1e-5
ATT_SCALE = (QK_NOPE + QK_ROPE) ** -0.5
LOG2E = 1.4426950408889634
Q_SCALE = ATT_SCALE * LOG2E
K_SCALE_M = DH_M ** -0.5
SEQ_LAG = 2
AHEAD = 4
PROJ_ROWS = 128

LANES = 128
QK_PAD = 256
VMEM_LIMIT = 56 * 1024 * 1024

NT_DIMS = (((1,), (1,)), ((), ()))


def _rms(x, n):
    ms = jnp.sum(x * x, axis=-1, keepdims=True) * (1.0 / n)
    return x * lax.rsqrt(ms + EPS)


def _silu(x):
    return x * jax.nn.sigmoid(x)


def _dot(a, b):
    return jnp.dot(a, b, preferred_element_type=f32)


def _dot_nt(a, b):
    return lax.dot_general(a, b, NT_DIMS, preferred_element_type=f32)


def _rep_lanes(x, n):
    return x if n == 1 else jnp.concatenate([x] * n, axis=1)


def _rope_table_kernel(ang_ref, cos_ref, sin_ref):
    ang = ang_ref[...]
    lane = lax.broadcasted_iota(jnp.int32, ang.shape, 1)
    cos_ref[...] = jnp.cos(ang)
    s = jnp.sin(ang)
    sin_ref[...] = jnp.where((lane % QK_ROPE) < QK_ROPE // 2, -s, s)


def _rope_tables(pos):
    half = QK_ROPE // 2
    inv = 1.0 / (ROPE_BASE ** (jnp.arange(0, QK_ROPE, 2, dtype=f32) / QK_ROPE))
    ang = pos.astype(f32)[:, None] * inv[None, :]
    ang = jnp.tile(ang, (1, LANES // half))
    s = ang.shape[0]
    return pl.pallas_call(
        _rope_table_kernel,
        out_shape=(jax.ShapeDtypeStruct((s, LANES), f32), jax.ShapeDtypeStruct((s, LANES), f32)),
        name="rope_tables",
    )(ang)


def _proj_kernel(x_ref, cos_ref, sin_ref, ng_ref, w1_ref, w2_ref, qg_ref, wuq_ref, kvg_ref, wuk_ref,
                 gqn_ref, gqr_ref, gkn_ref, gkr_ref,
                 qt_ref, k_ref, ckv_ref, ct_ref, kr_ref, sza_ref, min_ref, szm_ref):
    tm = x_ref.shape[1]
    rows = min(tm, PROJ_ROWS)
    tq = qt_ref.shape[4]
    tk = ct_ref.shape[3]
    lane = lax.broadcasted_iota(jnp.int32, (rows, LANES), 1)
    first_half = (lane % QK_ROPE) < QK_ROPE // 2

    for r0 in range(0, tm, rows):
        rs = slice(r0, r0 + rows)
        x = x_ref[0, rs, :]
        h = (_rms(x, D_MODEL) * ng_ref[...]).astype(bf16)
        p1 = _dot(h, w1_ref[...])
        p2 = _dot(h, w2_ref[...])
        cos = cos_ref[rs, :]
        sin = sin_ref[rs, :]

        def rope(xp):
            sw = jnp.where(first_half, pltpu.roll(xp, LANES - QK_ROPE // 2, 1), pltpu.roll(xp, QK_ROPE // 2, 1))
            return xp * cos + sw * sin

        ql = (_rms(p1[:, :Q_LORA], Q_LORA) * qg_ref[...]).astype(bf16)
        qf = _dot(ql, wuq_ref[...])
        for hh in range(N_HEADS_A):
            qn = _rms(qf[:, LANES * hh:LANES * (hh + 1)], QK_NOPE) * gqn_ref[...]
            o = N_HEADS_A * QK_NOPE + LANES * hh
            qr = rope(_rms(qf[:, o:o + LANES], QK_ROPE) * gqr_ref[...])
            qh = jnp.concatenate([qn, qr], axis=1) * Q_SCALE
            qt_ref[0, hh, r0 // tq, :, r0 % tq:r0 % tq + rows] = qh.T.astype(bf16)

        c = _rms(p1[:, Q_LORA:Q_LORA + KV_LORA], KV_LORA) * kvg_ref[...]
        ckv_ref[0, rs, :] = c
        cb = c.astype(bf16)
        ct_ref[0, r0 // tk, :, r0 % tk:r0 % tk + rows] = c.T.astype(bf16)
        kn = _dot(cb, wuk_ref[...])
        o = Q_LORA + KV_LORA
        krp = rope(_rms(p1[:, o:o + LANES], QK_ROPE) * gkr_ref[...])
        kr_ref[0, rs, :] = krp[:, :QK_ROPE]
        krb = krp.astype(bf16)
        for hh in range(N_HEADS_A):
            knh = _rms(kn[:, LANES * hh:LANES * (hh + 1)], QK_NOPE) * gkn_ref[...]
            k_ref[0, hh, rs, 0:LANES] = knh.astype(bf16)
            k_ref[0, hh, rs, LANES:QK_PAD] = krb

        sza_ref[0, rs, :] = _silu(p2[:, :D_ATT]).astype(bf16)
        min_ref[0, rs, :] = p2[:, D_ATT:D_ATT + D_MLS]
        szm_ref[0, rs, :] = _silu(p2[:, D_ATT + D_MLS:]).astype(bf16)


def _proj(x, cos, sin, wp, tm, tq, tk):
    nb, s, _ = x.shape
    grid = (nb, s // tm)
    full = lambda a: pl.BlockSpec(a.shape, lambda b, i: (0,) * a.ndim)
    tok = lambda w: pl.BlockSpec((1, tm, w), lambda b, i: (b, i, 0))
    head = pl.BlockSpec((1, N_HEADS_A, tm, QK_PAD), lambda b, i: (b, 0, i, 0))
    head_t = pl.BlockSpec((1, N_HEADS_A, tm // tq, QK_PAD, tq), lambda b, i: (b, 0, i, 0, 0))
    lat_t = pl.BlockSpec((1, tm // tk, KV_LORA, tk), lambda b, i: (b, i, 0, 0))
    tab = pl.BlockSpec((tm, LANES), lambda b, i: (i, 0))
    weights = (wp["norm_g"], wp["w1"], wp["w2"], wp["q_norm_g"], wp["w_uq"], wp["kv_norm_g"], wp["w_uk"],
               wp["g_qn"], wp["g_qr"], wp["g_kn"], wp["g_kr"])
    out_shape = (
        jax.ShapeDtypeStruct((nb, N_HEADS_A, s // tq, QK_PAD, tq), bf16),
        jax.ShapeDtypeStruct((nb, N_HEADS_A, s, QK_PAD), bf16),
        jax.ShapeDtypeStruct((nb, s, KV_LORA), f32),
        jax.ShapeDtypeStruct((nb, s // tk, KV_LORA, tk), bf16),
        jax.ShapeDtypeStruct((nb, s, QK_ROPE), f32),
        jax.ShapeDtypeStruct((nb, s, D_ATT), bf16),
        jax.ShapeDtypeStruct((nb, s, D_MLS), f32),
        jax.ShapeDtypeStruct((nb, s, D_MLS), bf16),
    )
    out_specs = (head_t, head, tok(KV_LORA), lat_t, tok(QK_ROPE), tok(D_ATT), tok(D_MLS), tok(D_MLS))
    return pl.pallas_call(
        _proj_kernel,
        grid=grid,
        in_specs=[tok(D_MODEL), tab, tab] + [full(w) for w in weights],
        out_specs=out_specs,
        out_shape=out_shape,
        compiler_params=pltpu.CompilerParams(dimension_semantics=("arbitrary", "arbitrary"),
                                             vmem_limit_bytes=VMEM_LIMIT),
        name="in_proj",
    )(x, cos, sin, *weights)


def _attn_kernel(qt_ref, k_ref, ct_ref, sza_ref, wuv_ref, x_ref, mm_ref, woa_ref, wom_ref, y_ref,
                 m_sc, l_sc, acc_sc, s_sc, mixa_sc, *, tq, tk):
    i = pl.program_id(1)
    ratio = tq // tk
    kv_idx = lax.broadcasted_iota(jnp.int32, (tk, tq), 0)
    q_idx = lax.broadcasted_iota(jnp.int32, (tk, tq), 1)

    m_sc[...] = jnp.full(m_sc.shape, -jnp.inf, f32)
    l_sc[...] = jnp.zeros(l_sc.shape, f32)
    acc_sc[...] = jnp.zeros(acc_sc.shape, f32)

    def step(j, diag):
        start = pl.multiple_of(j * tk, tk)
        ctj = ct_ref[0, j]

        def scores(hh):
            s = _dot(k_ref[0, hh, pl.ds(start, tk), :], qt_ref[0, hh, 0])
            s_sc[hh] = s if diag is None else jnp.where(kv_idx + diag * tk <= q_idx, s, -jnp.inf)

        def update(hh):
            s = s_sc[hh]
            m_old = m_sc[hh]
            m_new = jnp.maximum(m_old, jnp.max(s, axis=0, keepdims=True))
            alpha = jnp.exp2(m_old - m_new)
            p = jnp.exp2(s - m_new[0:1, :])
            l_sc[hh] = alpha * l_sc[hh] + jnp.sum(p, axis=0, keepdims=True)
            acc_sc[hh] = alpha[0:1, :] * acc_sc[hh] + _dot(ctj, p.astype(bf16))
            m_sc[hh] = m_new

        for hh in range(min(AHEAD, N_HEADS_A)):
            scores(hh)
        for hh in range(N_HEADS_A):
            if hh + AHEAD < N_HEADS_A:
                scores(hh + AHEAD)
            update(hh)

    def body(j, carry):
        step(j, None)
        return carry

    lax.fori_loop(0, ratio * i, body, 0)
    for d in range(ratio):
        step(ratio * i + d, d)

    for hh in range(N_HEADS_A):
        inv = 1.0 / l_sc[hh]
        ctx = (acc_sc[hh] * inv[0:1, :]).T.astype(bf16)
        att = _dot(ctx, wuv_ref[hh])
        sl = slice(V_HEAD * hh, V_HEAD * (hh + 1))
        mixa_sc[:, sl] = (att * sza_ref[0, :, sl].astype(f32)).astype(bf16)
    y_ref[0] = x_ref[0] + _dot(mixa_sc[...], woa_ref[...]) + _dot(mm_ref[0], wom_ref[...])


def _attn_prompt(qt, k, ct, sza, x, mixm, wp, tq, tk):
    nb, _, s, _ = k.shape
    assert tq % tk == 0
    assert qt.shape == (nb, N_HEADS_A, s // tq, QK_PAD, tq) and ct.shape == (nb, s // tk, KV_LORA, tk)
    full = lambda a: pl.BlockSpec(a.shape, lambda b, i: (0,) * a.ndim)
    tok = lambda w: pl.BlockSpec((1, tq, w), lambda b, i: (b, i, 0))
    return pl.pallas_call(
        functools.partial(_attn_kernel, tq=tq, tk=tk),
        grid=(nb, s // tq),
        in_specs=[
            pl.BlockSpec((1, N_HEADS_A, 1, QK_PAD, tq), lambda b, i: (b, 0, i, 0, 0)),
            pl.BlockSpec((1, N_HEADS_A, s, QK_PAD), lambda b, i: (b, 0, 0, 0)),
            pl.BlockSpec((1, s // tk, KV_LORA, tk), lambda b, i: (b, 0, 0, 0)),
            tok(D_ATT), full(wp["w_uv"]), tok(D_MODEL), tok(D_MLS), full(wp["w_out_a"]), full(wp["w_out_m"]),
        ],
        out_specs=tok(D_MODEL),
        out_shape=jax.ShapeDtypeStruct((nb, s, D_MODEL), f32),
        scratch_shapes=[pltpu.VMEM((N_HEADS_A, 8, tq), f32),
                        pltpu.VMEM((N_HEADS_A, 8, tq), f32),
                        pltpu.VMEM((N_HEADS_A, KV_LORA, tq), f32),
                        pltpu.VMEM((N_HEADS_A, tk, tq), f32),
                        pltpu.VMEM((tq, D_ATT), bf16)],
        compiler_params=pltpu.CompilerParams(dimension_semantics=("arbitrary", "arbitrary"),
                                             vmem_limit_bytes=VMEM_LIMIT),
        name="attn_prompt",
    )(qt, k, ct, sza, wp["w_uv"], x, mixm, wp["w_out_a"], wp["w_out_m"])


def _scan_lanes(x, op, fill, length):
    lane = lax.broadcasted_iota(jnp.int32, x.shape, 1) % length
    d = 1
    while d < length:
        x = op(x, jnp.where(lane >= d, pltpu.roll(x, d, 1), fill))
        d *= 2
    return x


def _mlstm_kernel(min_ref, szm_ref, cw_ref, cb_ref, wqk_ref, wv_ref, wg_ref, gb_ref, lng_ref, skip_ref, sel_ref,
                  mix_ref, cn_ref, m_ref, halo_ref,
                  xbuf, ca_sc, qkv_sc, ks_sc, cn_sc, m_sc, cols_sc, inter_sc, sqk_sc, *, ts, lc, ne):
    si = pl.program_id(1)
    ns = pl.num_programs(1)
    nc = ts // lc

    @pl.when(si == 0)
    def _():
        xbuf[:, 0:8, :] = jnp.zeros((ne, 8, D_MLS), f32)
        cn_sc[...] = jnp.zeros(cn_sc.shape, f32)
        m_sc[...] = jnp.zeros(m_sc.shape, f32)

    row = lax.broadcasted_iota(jnp.int32, (lc, lc), 0)
    col = lax.broadcasted_iota(jnp.int32, (lc, lc), 1)
    causal = col <= row
    ones_b = jnp.ones((lc, LANES), bf16)

    def projections(e):
        x = min_ref[e]
        xbuf[e, 8:8 + ts, :] = x
        y = cb_ref[...] + xbuf[e, 8:8 + ts, :] * cw_ref[3:4, :]
        for j in range(CONV_W - 1):
            y = y + xbuf[e, 5 + j:5 + j + ts, :] * cw_ref[j:j + 1, :]
        xbuf[e, 0:8, :] = xbuf[e, ts:ts + 8, :]
        ca = _silu(y)
        ca_sc[e] = ca
        cab = ca.astype(bf16)
        xb = x.astype(bf16)
        for hh in range(N_HEADS_M):
            sl = slice(DH_M * hh, DH_M * (hh + 1))
            qk = _dot(cab[:, sl], wqk_ref[hh])
            qkv_sc[e, :, sl] = qk[:, :DH_M].astype(bf16)
            kh = qk[:, DH_M:]
            qkv_sc[e, :, D_MLS + DH_M * hh:D_MLS + DH_M * (hh + 1)] = kh.astype(bf16)
            ks_sc[e, :, sl] = (kh * K_SCALE_M).astype(bf16)
            qkv_sc[e, :, 2 * D_MLS + DH_M * hh:2 * D_MLS + DH_M * (hh + 1)] = (
                _dot(xb[:, sl], wv_ref[hh]).astype(bf16))

    def gate_scans(e):
        g = _dot(qkv_sc[e], wg_ref[...]) + gb_ref[...]
        gt = g.T[0:8, :]
        head_row = lax.broadcasted_iota(jnp.int32, gt.shape, 0) < N_HEADS_M
        li_all = jnp.where(head_row, gt, 0.0)
        lf_all = jnp.where(head_row, jax.nn.log_sigmoid(pltpu.roll(gt, 4, 0)), 0.0)
        for c in range(nc):
            for hh in range(N_HEADS_M):
                cs = slice(lc * c, lc * (c + 1))
                sl = slice(DH_M * hh, DH_M * (hh + 1))
                sqk_sc[e, c, hh] = _dot_nt(qkv_sc[e, cs, sl], ks_sc[e, cs, sl])
        b_all = _scan_lanes(lf_all, jnp.add, 0.0, lc)
        u_all = li_all - b_all
        cm_all = _scan_lanes(u_all, jnp.maximum, -jnp.inf, lc)
        u_max = [jnp.max(u_all[:, lc * c:lc * (c + 1)], axis=1, keepdims=True) for c in range(nc)]
        lf_sum = [jnp.sum(lf_all[:, lc * c:lc * (c + 1)], axis=1, keepdims=True) for c in range(nc)]
        return b_all, u_all, cm_all, u_max, lf_sum

    def chunk_refs(e, c, hh):
        cs = slice(lc * c, lc * (c + 1))
        sl = slice(DH_M * hh, DH_M * (hh + 1))
        vc = qkv_sc[e, cs, 2 * D_MLS + DH_M * hh:2 * D_MLS + DH_M * (hh + 1)]
        v_aug = jnp.concatenate([vc, ones_b], axis=1)
        return cs, sl, qkv_sc[e, cs, sl], ks_sc[e, cs, sl], v_aug

    def state_pass(e, c, scans, m0):
        b_all, u_all, cm_all, u_max, lf_sum = scans
        cs = slice(lc * c, lc * (c + 1))
        u = u_all[:, cs]
        gmax = jnp.maximum(m0, cm_all[:, cs])
        mt = b_all[:, cs] + gmax
        g_last = jnp.maximum(m0, u_max[c])
        m_new = lf_sum[c] + g_last
        decay = jnp.exp(m0 - g_last)
        w_s = jnp.exp(u - g_last)
        r = jnp.concatenate([gmax, mt], axis=0)
        r_hi = r.astype(bf16).astype(f32)
        r_mid = (r - r_hi).astype(bf16).astype(f32)
        r_lo = r - r_hi - r_mid
        stack = jnp.concatenate([r_hi, r_mid, r_lo, jnp.zeros((lc - 48, lc), f32)], axis=0)
        cols_sc[e, c] = _dot(stack.T.astype(bf16), sel_ref[...])
        for hh in range(N_HEADS_M):
            _, _, qc, kc, v_aug = chunk_refs(e, c, hh)
            inter_sc[e, c, hh] = _dot(qc, cn_sc[e, hh].astype(bf16))
            ktw = (kc.astype(f32).T * w_s[hh:hh + 1, :]).astype(bf16)
            upd = _dot(ktw, v_aug)
            dec = jnp.broadcast_to(decay[hh:hh + 1, :], (DH_M, LANES))
            cn_sc[e, hh] = _rep_lanes(dec, 2) * cn_sc[e, hh] + upd
        return u, m_new

    def output_pass(e, c, u, m0):
        for hh in range(N_HEADS_M):
            cs, sl, _, _, v_aug = chunk_refs(e, c, hh)
            g_col = cols_sc[e, c, :, LANES * hh:LANES * (hh + 1)]
            mt_col = cols_sc[e, c, :, LANES * (N_HEADS_M + hh):LANES * (N_HEADS_M + hh + 1)]
            wm = jnp.where(causal, jnp.exp(u[hh:hh + 1, :] - g_col), 0.0)
            p = (sqk_sc[e, c, hh] * wm).astype(bf16)
            intra = _dot(p, v_aug)
            w_inter = jnp.exp(m0[hh:hh + 1, :] - g_col)
            tot = _rep_lanes(w_inter, 2) * inter_sc[e, c, hh] + intra
            num = tot[:, :DH_M]
            den = tot[:, DH_M:]
            hv = num / jnp.maximum(jnp.abs(den), jnp.exp(-mt_col))
            mu = jnp.mean(hv, axis=-1, keepdims=True)
            hc = hv - mu
            var = jnp.mean(hc * hc, axis=-1, keepdims=True)
            hn = hc * lax.rsqrt(var + LN_EPS) * lng_ref[:, sl]
            out = hn + skip_ref[:, sl] * ca_sc[e, cs, sl]
            mix_ref[e, cs, sl] = (out * szm_ref[e, cs, sl].astype(f32)).astype(bf16)

    seqs = range(ne)
    scans = []
    for e in seqs:
        projections(e)
        scans.append(gate_scans(e))
    m_cur = [m_sc[e] for e in seqs]
    staged = [[] for _ in seqs]
    for t in range(nc + 1 + SEQ_LAG * (ne - 1)):
        for e in seqs:
            c = t - SEQ_LAG * e
            if 0 <= c < nc:
                u_c, m_next = state_pass(e, c, scans[e], m_cur[e])
                staged[e].append((u_c, m_cur[e]))
                m_cur[e] = m_next
            if 1 <= c <= nc:
                output_pass(e, c - 1, *staged[e][c - 1])
    for e in seqs:
        m_sc[e] = m_cur[e]

    @pl.when(si == ns - 1)
    def _():
        cn_ref[...] = cn_sc[...]
        m_ref[...] = m_sc[...]
        halo_ref[...] = xbuf[:, 0:8, :]


def _column_selector():
    rows = lax.broadcasted_iota(jnp.int32, (LANES, 2 * N_HEADS_M * LANES), 0)
    block = lax.broadcasted_iota(jnp.int32, (LANES, 2 * N_HEADS_M * LANES), 1) // LANES
    src = 8 * (block // N_HEADS_M) + block % N_HEADS_M
    return ((rows < 48) & (rows % 16 == src)).astype(bf16)


def _mlstm_prompt(m_in, szm, wp, ts, lc):
    nb, s, _ = m_in.shape
    full = lambda a: pl.BlockSpec(a.shape, lambda b, i: (0,) * a.ndim)
    ne = 2 if nb % 2 == 0 else 1
    tok = pl.BlockSpec((ne, ts, D_MLS), lambda b, i: (b, i, 0))
    assert lc == LANES, "one chunk spans exactly one vreg of lanes"
    weights = (wp["conv_w"], wp["conv_b"], wp["w_qk_m"], wp["w_v_m"], wp["w_gate"], wp["gate_b"],
               wp["mh_norm_g"], wp["skip"], _column_selector())
    nc = ts // lc
    return pl.pallas_call(
        functools.partial(_mlstm_kernel, ts=ts, lc=lc, ne=ne),
        grid=(nb // ne, s // ts),
        in_specs=[tok, tok] + [full(w) for w in weights],
        out_specs=(tok,
                   pl.BlockSpec((ne, N_HEADS_M, DH_M, 2 * DH_M), lambda b, i: (b, 0, 0, 0)),
                   pl.BlockSpec((ne, 8, LANES), lambda b, i: (b, 0, 0)),
                   pl.BlockSpec((ne, 8, D_MLS), lambda b, i: (b, 0, 0))),
        out_shape=(jax.ShapeDtypeStruct((nb, s, D_MLS), bf16),
                   jax.ShapeDtypeStruct((nb, N_HEADS_M, DH_M, 2 * DH_M), f32),
                   jax.ShapeDtypeStruct((nb, 8, LANES), f32),
                   jax.ShapeDtypeStruct((nb, 8, D_MLS), f32)),
        scratch_shapes=[pltpu.VMEM((ne, ts + 8, D_MLS), f32),
                        pltpu.VMEM((ne, ts, D_MLS), f32),
                        pltpu.VMEM((ne, ts, 3 * D_MLS), bf16),
                        pltpu.VMEM((ne, ts, D_MLS), bf16),
                        pltpu.VMEM((ne, N_HEADS_M, DH_M, 2 * DH_M), f32),
                        pltpu.VMEM((ne, 8, LANES), f32),
                        pltpu.VMEM((ne, nc, lc, 2 * N_HEADS_M * LANES), f32),
                        pltpu.VMEM((ne, nc, N_HEADS_M, lc, 2 * DH_M), f32),
                        pltpu.VMEM((ne, nc, N_HEADS_M, lc, lc), f32)],
        compiler_params=pltpu.CompilerParams(dimension_semantics=("arbitrary", "arbitrary"),
                                             vmem_limit_bytes=VMEM_LIMIT),
        name="mlstm_prompt",
    )(m_in, szm, *weights)


def _sattn_kernel(pt_ref, q_ref, knew_ref, cnew_ref, wukt_ref, gkn_ref, ckv_hbm, krt_hbm, ctx_ref,
                  cbuf0, cbuf1, kbuf0, kbuf1, sem, cb_sc, knt_sc, lhs_sc, *, layer, pb, sub):
    b = pl.program_id(0)
    nbatch = pl.num_programs(0)
    npages = pt_ref.shape[1]
    nblk = npages // pb
    tb = pb * PAGE_SIZE
    nsub = tb // sub
    sub_pages = sub // PAGE_SIZE
    cbufs = (cbuf0, cbuf1)
    kbufs = (kbuf0, kbuf1)

    def part_copies(bb, blk, slot, si):
        latent, ropek = [], []
        for p in range(sub_pages * si, sub_pages * (si + 1)):
            page = pt_ref[bb, blk * pb + p]
            rows = pl.ds(p * PAGE_SIZE, PAGE_SIZE)
            latent.append(pltpu.make_async_copy(ckv_hbm.at[layer, page], cbufs[slot].at[rows], sem.at[slot, si, 0]))
            ropek.append(pltpu.make_async_copy(krt_hbm.at[layer, page], kbufs[slot].at[p], sem.at[slot, si, 1]))
        return latent + ropek

    def start_part(bb, blk, slot, si):
        for cp in part_copies(bb, blk, slot, si):
            cp.start()

    def wait_part(bb, blk, slot, si):
        for cp in part_copies(bb, blk, slot, si):
            cp.wait()

    def two_ahead(blk):
        wraps = blk + 2 >= nblk
        return jnp.where(wraps, jnp.minimum(b + 1, nbatch - 1), b), jnp.where(wraps, blk + 2 - nblk, blk + 2)

    @pl.when(b == 0)
    def _():
        for slot in range(2):
            for si in range(nsub):
                start_part(0, slot, slot, si)

    q = q_ref[0]
    qg = q[:, :QK_NOPE].astype(f32) * gkn_ref[...]
    rowq = lax.broadcasted_iota(jnp.int32, (16, QK_NOPE), 0)
    qabs = jnp.zeros((16, KV_LORA), f32)
    for hh in range(N_HEADS_A):
        lhs = jnp.where(rowq == hh, jnp.broadcast_to(qg[hh:hh + 1, :], (16, QK_NOPE)), 0.0).astype(bf16)
        qabs = qabs + _dot(lhs, wukt_ref[QK_NOPE * hh:QK_NOPE * (hh + 1), :])
    nk = N_HEADS_A * QK_NOPE
    lhs_sc[0:nk, :] = wukt_ref[...]
    lhs_sc[nk:nk + 16, :] = qabs.astype(bf16)
    qr16 = jnp.concatenate([q[:, QK_NOPE:QK_NOPE + QK_ROPE],
                            jnp.zeros((16 - N_HEADS_A, QK_ROPE), bf16)], axis=0)

    row16 = lax.broadcasted_iota(jnp.int32, (16, sub), 0)

    def matmuls(slot, si):
        cb = cbufs[slot][sub * si:sub * (si + 1), :].astype(bf16)
        cb_sc[si % 2] = cb
        knt_sc[si % 2] = _dot_nt(lhs_sc[...], cb)
        srope = jnp.concatenate(
            [_dot(qr16, kbufs[slot][sub_pages * si + p].astype(bf16)) for p in range(sub_pages)],
            axis=1)
        return srope

    def softmax_update(si, srope, carry):
        num = knt_sc[si % 2, nk:nk + 16, :]
        m, l, acc = carry
        rnorm = jnp.zeros((16, sub), f32)
        for hh in range(N_HEADS_A):
            kh = knt_sc[si % 2, QK_NOPE * hh:QK_NOPE * (hh + 1), :]
            ss = jnp.sum(kh * kh, axis=0, keepdims=True)
            rnorm = jnp.where(row16 == hh, lax.rsqrt(ss * (1.0 / QK_NOPE) + EPS), rnorm)
        s = num * rnorm + srope
        m_new = jnp.maximum(m, jnp.max(s, axis=1, keepdims=True))
        alpha = jnp.exp2(m - m_new)
        p = jnp.exp2(s - _rep_lanes(m_new, sub // LANES))
        l = alpha * l + jnp.sum(p, axis=1, keepdims=True)
        acc = _rep_lanes(alpha, KV_LORA // LANES) * acc + _dot(p.astype(bf16), cb_sc[si % 2])
        return m_new, l, acc

    def compute(slot, blk, carry):
        nxt_b, nxt_blk = two_ahead(blk)
        for si in range(nsub):
            wait_part(b, blk, slot, si)

        def stage(si):
            staged = matmuls(slot, si)
            start_part(nxt_b, nxt_blk, slot, si)
            return staged

        staged = stage(0)
        for si in range(nsub):
            nxt = stage(si + 1) if si + 1 < nsub else None
            carry = softmax_update(si, staged, carry)
            staged = nxt
        return carry

    def pair(i, carry):
        return compute(1, 2 * i + 1, compute(0, 2 * i, carry))

    init = (jnp.full((16, LANES), -jnp.inf, f32), jnp.zeros((16, LANES), f32), jnp.zeros((16, KV_LORA), f32))
    m_old, l_old, acc_old = lax.fori_loop(0, nblk // 2, pair, init)

    @pl.when(b == nbatch - 1)
    def _():
        for slot in range(2):
            for si in range(nsub):
                wait_part(b, slot, slot, si)

    kn = knew_ref[0].astype(f32)
    s_new = jnp.sum(q.astype(f32) * kn, axis=1, keepdims=True)
    s_new = jnp.concatenate([jnp.broadcast_to(s_new, (N_HEADS_A, LANES)),
                             jnp.zeros((16 - N_HEADS_A, LANES), f32)], axis=0)
    m_new = jnp.maximum(m_old, s_new)
    alpha = jnp.exp2(m_old - m_new)
    p_new = jnp.exp2(s_new - m_new)
    l = alpha * l_old + p_new
    acc = _rep_lanes(alpha, 2) * acc_old + _rep_lanes(p_new, 2) * cnew_ref[0]
    ctx = acc * _rep_lanes(1.0 / l, 2)
    ctx_ref[0] = ctx[0:N_HEADS_A, :]


def _attn_sample(page_table, q_s, k_s, c_s, wp, cache_ckv, cache_krope_t, layer, pb, sub):
    nbd = q_s.shape[0]
    tb = pb * PAGE_SIZE
    full = lambda a: pl.BlockSpec(a.shape, lambda b, pt: (0,) * a.ndim)
    grid_spec = pltpu.PrefetchScalarGridSpec(
        num_scalar_prefetch=1,
        grid=(nbd,),
        in_specs=[pl.BlockSpec((1, N_HEADS_A, QK_PAD), lambda b, pt: (b, 0, 0)),
                  pl.BlockSpec((1, N_HEADS_A, QK_PAD), lambda b, pt: (b, 0, 0)),
                  pl.BlockSpec((1, 1, KV_LORA), lambda b, pt: (b, 0, 0)),
                  full(wp["w_uk_t"]), full(wp["g_kn"]),
                  pl.BlockSpec(memory_space=pl.ANY),
                  pl.BlockSpec(memory_space=pl.ANY)],
        out_specs=pl.BlockSpec((1, N_HEADS_A, KV_LORA), lambda b, pt: (b, 0, 0)),
        scratch_shapes=[pltpu.VMEM((tb, KV_LORA), f32),
                        pltpu.VMEM((tb, KV_LORA), f32),
                        pltpu.VMEM((pb, QK_ROPE, PAGE_SIZE), f32),
                        pltpu.VMEM((pb, QK_ROPE, PAGE_SIZE), f32),
                        pltpu.SemaphoreType.DMA((2, tb // sub, 2)),
                        pltpu.VMEM((2, sub, KV_LORA), bf16),
                        pltpu.VMEM((2, N_HEADS_A * QK_NOPE + 16, sub), f32),
                        pltpu.VMEM((N_HEADS_A * QK_NOPE + 16, KV_LORA), bf16)],
    )
    return pl.pallas_call(
        functools.partial(_sattn_kernel, layer=layer, pb=pb, sub=sub),
        grid_spec=grid_spec,
        out_shape=jax.ShapeDtypeStruct((nbd, N_HEADS_A, KV_LORA), f32),
        compiler_params=pltpu.CompilerParams(dimension_semantics=("arbitrary",), vmem_limit_bytes=VMEM_LIMIT),
        name="attn_sample",
    )(page_table, q_s, k_s, c_s, wp["w_uk_t"], wp["g_kn"], cache_ckv, cache_krope_t)


def _stail_kernel(x_ref, ctx_ref, sza_ref, min_ref, szm_ref, conv_ref, c0_ref, n0_ref, m0_ref,
                  wuv_ref, cw_ref, cb_ref, wqk_ref, wv_ref, wg_ref, gb_ref, lng_ref, skip_ref, woa_ref, wom_ref,
                  y_ref, c1_ref, n1_ref, m1_ref, conv1_ref, *, bb):
    mixa = []
    for hh in range(N_HEADS_A):
        att = _dot(ctx_ref[hh].astype(bf16), wuv_ref[hh])
        sl = slice(V_HEAD * hh, V_HEAD * (hh + 1))
        mixa.append((att * sza_ref[:, sl].astype(f32)).astype(bf16))
    mixa = jnp.concatenate(mixa, axis=1)

    x = min_ref[...]
    y = cb_ref[...] + x * cw_ref[3:4, :]
    for j in range(CONV_W - 1):
        y = y + conv_ref[j] * cw_ref[j:j + 1, :]
    for j in range(CONV_W - 2):
        conv1_ref[j] = conv_ref[j + 1]
    conv1_ref[CONV_W - 2] = x
    ca = _silu(y)
    cab = ca.astype(bf16)
    xb = x.astype(bf16)
    qs, ks, vs = [], [], []
    for hh in range(N_HEADS_M):
        sl = slice(DH_M * hh, DH_M * (hh + 1))
        qk = _dot(cab[:, sl], wqk_ref[hh])
        qs.append(qk[:, :DH_M])
        ks.append(qk[:, DH_M:])
        vs.append(_dot(xb[:, sl], wv_ref[hh]))
    qkv = jnp.concatenate(qs + ks + vs, axis=1).astype(bf16)
    g = _dot(qkv, wg_ref[...]) + gb_ref[...]
    li = g
    lf = jax.nn.log_sigmoid(pltpu.roll(g, LANES - N_HEADS_M, 1))
    m0 = m0_ref[...]
    a = lf + m0
    mt = jnp.maximum(a, li)
    w_inter = jnp.exp(a - mt)
    w_new = jnp.exp(li - mt)
    emt = jnp.exp(-mt)
    m1_ref[...] = mt

    rowi = lax.broadcasted_iota(jnp.int32, (bb, DH_M), 0)
    rowp = lax.broadcasted_iota(jnp.int32, (LANES, DH_M), 0)
    zpad = jnp.zeros((LANES - bb, DH_M), f32)
    mixm = []
    for hh in range(N_HEADS_M):
        sl = slice(DH_M * hh, DH_M * (hh + 1))
        col = lambda z: jnp.broadcast_to(z[:, hh:hh + 1], (bb, DH_M))
        wi, wn, em = col(w_inter), col(w_new), col(emt)
        qh = qs[hh]
        kh = ks[hh] * K_SCALE_M
        vh = vs[hh]
        qb_ = qh.astype(bf16)
        kw = (kh * wn)
        kwt = jnp.concatenate([kw, zpad], axis=0).T.astype(bf16)
        vb_ = vh.astype(bf16)
        vpad = jnp.concatenate([vh, zpad], axis=0)
        inter = jnp.zeros((bb, DH_M), f32)
        for r in range(bb):
            c0 = c0_ref[r, hh]
            inter = inter + jnp.where(rowi == r, _dot(qb_, c0.astype(bf16)), 0.0)
            upd = _dot(kwt, jnp.where(rowp == r, vpad, 0.0).astype(bf16))
            dec = jnp.broadcast_to(wi[r:r + 1, :], (DH_M, DH_M))
            c1_ref[r, hh] = dec * c0 + upd
        n0 = n0_ref[hh]
        n1_ref[hh] = wi * n0 + kw
        qk_dot = jnp.sum(qb_.astype(f32) * kh.astype(bf16).astype(f32), axis=1, keepdims=True) * wn
        num = wi * inter + qk_dot * vb_.astype(f32)
        den = wi * jnp.sum(qb_.astype(f32) * n0.astype(bf16).astype(f32), axis=1, keepdims=True) + qk_dot
        hv = num / jnp.maximum(jnp.abs(den), em)
        mu = jnp.mean(hv, axis=-1, keepdims=True)
        hc = hv - mu
        var = jnp.mean(hc * hc, axis=-1, keepdims=True)
        hn = hc * lax.rsqrt(var + LN_EPS) * lng_ref[:, sl]
        out = hn + skip_ref[:, sl] * ca[:, sl]
        mixm.append((out * szm_ref[:, sl].astype(f32)).astype(bf16))
    mixm = jnp.concatenate(mixm, axis=1)
    y_ref[...] = x_ref[...] + _dot(mixa, woa_ref[...]) + _dot(mixm, wom_ref[...])


def _sample_tail(x_s, ctx, sza, m_in, szm, conv0, c0, n0, m0, wp, bb):
    nbd = x_s.shape[0]
    full = lambda a: pl.BlockSpec(a.shape, lambda i: (0,) * a.ndim)
    row = lambda w: pl.BlockSpec((bb, w), lambda i: (i, 0))
    weights = (wp["w_uv"], wp["conv_w"], wp["conv_b"], wp["w_qk_m"], wp["w_v_m"], wp["w_gate"], wp["gate_b"],
               wp["mh_norm_g"], wp["skip"], wp["w_out_a"], wp["w_out_m"])
    conv_spec = pl.BlockSpec((CONV_W - 1, bb, D_MLS), lambda i: (0, i, 0))
    c_spec = pl.BlockSpec((bb, N_HEADS_M, DH_M, DH_M), lambda i: (i, 0, 0, 0))
    n_spec = pl.BlockSpec((N_HEADS_M, bb, DH_M), lambda i: (0, i, 0))
    return pl.pallas_call(
        functools.partial(_stail_kernel, bb=bb),
        grid=(nbd // bb,),
        in_specs=[row(D_MODEL),
                  pl.BlockSpec((N_HEADS_A, bb, KV_LORA), lambda i: (0, i, 0)),
                  row(D_ATT), row(D_MLS), row(D_MLS), conv_spec, c_spec, n_spec, row(LANES)]
                 + [full(w) for w in weights],
        out_specs=(row(D_MODEL), c_spec, n_spec, row(LANES), conv_spec),
        out_shape=(jax.ShapeDtypeStruct((nbd, D_MODEL), f32),
                   jax.ShapeDtypeStruct(c0.shape, f32),
                   jax.ShapeDtypeStruct(n0.shape, f32),
                   jax.ShapeDtypeStruct((nbd, LANES), f32),
                   jax.ShapeDtypeStruct(conv0.shape, f32)),
        compiler_params=pltpu.CompilerParams(dimension_semantics=("arbitrary",), vmem_limit_bytes=VMEM_LIMIT),
        name="sample_tail",
    )(x_s, ctx, sza, m_in, szm, conv0, c0, n0, m0, *weights)


def _prep_weights(l, norm_g, w_in, q_norm_g, w_uq, kv_norm_g, w_uk, w_uv, g_qn, g_qr, g_kn, g_kr,
                  conv_w, conv_b, w_q_m, w_k_m, w_v_m, w_gate, b_i, b_f, mh_norm_g, skip, w_out):
    wi = w_in[l]
    o1 = Q_LORA + KV_LORA + QK_ROPE
    w1 = jnp.concatenate([wi[:, :o1], jnp.zeros((D_MODEL, LANES - QK_ROPE), f32)], axis=1)
    w2 = wi[:, o1:]
    wq = w_uq[l].reshape(Q_LORA, N_HEADS_A, QK_NOPE + QK_ROPE)
    wq_rope = jnp.concatenate([wq[:, :, QK_NOPE:], jnp.zeros((Q_LORA, N_HEADS_A, LANES - QK_ROPE), f32)], axis=2)
    wuq = jnp.concatenate([wq[:, :, :QK_NOPE].reshape(Q_LORA, -1), wq_rope.reshape(Q_LORA, -1)], axis=1)
    pad_rope = lambda g: jnp.concatenate([g, jnp.zeros((LANES - QK_ROPE,), f32)])[None, :]
    wuk = w_uk[l].reshape(KV_LORA, N_HEADS_A * QK_NOPE)
    wg = jnp.concatenate([w_gate[l], jnp.zeros((3 * D_MLS, LANES - 2 * N_HEADS_M), f32)], axis=1)
    gate_b = jnp.concatenate([b_i[l], b_f[l], jnp.zeros((LANES - 2 * N_HEADS_M,), f32)])[None, :]
    return {
        "norm_g": norm_g[l][None, :],
        "w1": w1.astype(bf16),
        "w2": w2.astype(bf16),
        "q_norm_g": q_norm_g[l][None, :],
        "w_uq": wuq.astype(bf16),
        "kv_norm_g": kv_norm_g[l][None, :],
        "w_uk": wuk.astype(bf16),
        "w_uk_t": wuk.T.astype(bf16),
        "w_uv": jnp.transpose(w_uv[l], (1, 0, 2)).astype(bf16),
        "g_qn": g_qn[l][None, :],
        "g_qr": pad_rope(g_qr[l]),
        "g_kn": g_kn[l][None, :],
        "g_kr": pad_rope(g_kr[l]),
        "conv_w": conv_w[l],
        "conv_b": conv_b[l][None, :],
        "w_qk_m": jnp.concatenate([w_q_m[l], w_k_m[l]], axis=2).astype(bf16),
        "w_v_m": w_v_m[l].astype(bf16),
        "w_gate": wg.astype(bf16),
        "gate_b": gate_b,
        "mh_norm_g": mh_norm_g[l][None, :],
        "skip": skip[l][None, :],
        "w_out_a": w_out[l][:D_ATT].astype(bf16),
        "w_out_m": w_out[l][D_ATT:].astype(bf16),
    }


def _pick(n, candidates):
    for c in candidates:
        if n % c == 0:
            return c
    raise ValueError(f"no tile size for extent {n}")


def kernel(x_prompt, x_sample, cache_ckv, cache_krope, state_C, state_n, state_m, state_conv, page_table,
           norm_g, w_in, q_norm_g, w_uq, kv_norm_g, w_uk, w_uv, g_qn, g_qr, g_kn, g_kr,
           conv_w, conv_b, w_q_m, w_k_m, w_v_m, w_gate, b_i, b_f, mh_norm_g, skip, w_out):
    nb, s, _ = x_prompt.shape
    nbd, sd, _ = x_sample.shape
    depth = norm_g.shape[0]
    assert sd == 1, "sample path handles one new token per sequence"
    npages = page_table.shape[1]
    past_len = npages * PAGE_SIZE

    cos_p, sin_p = _rope_tables(jnp.arange(s))
    cos_s, sin_s = _rope_tables(jnp.full((nbd,), past_len, jnp.int32))

    tm_p = _pick(s, (512, 256, 128))
    tk_att = _pick(s, (256, 128))
    tq_att = _pick(tm_p, (2 * tk_att, tk_att))
    ts_m = _pick(s, (512, 256, 128))
    assert npages % 2 == 0, "sample attention double-buffers an even number of page blocks"
    pb = _pick(npages // 2, (32, 16, 8, 4, 2, 1))
    sub = _pick(pb * PAGE_SIZE, (2048, 1024, 512, 256, 128))
    bb = _pick(nbd, (16,))

    cache_krope_t = jnp.swapaxes(cache_krope, 2, 3)

    yp = x_prompt
    ys = x_sample.reshape(1, nbd, D_MODEL)
    outs_p, outs_s = [], []
    for l in range(depth):
        wp = _prep_weights(l, norm_g, w_in, q_norm_g, w_uq, kv_norm_g, w_uk, w_uv, g_qn, g_qr, g_kn, g_kr,
                           conv_w, conv_b, w_q_m, w_k_m, w_v_m, w_gate, b_i, b_f, mh_norm_g, skip, w_out)
        qt, k, ckv, ct, kr, sza, m_in, szm = _proj(yp, cos_p, sin_p, wp, tm_p, tq_att, tk_att)
        mixm, cn, mm, halo = _mlstm_prompt(m_in, szm, wp, ts_m, LANES)
        yp = _attn_prompt(qt, k, ct, sza, yp, mixm, wp, tq_att, tk_att)
        outs_p.append((ckv, kr, cn[..., :DH_M], cn[..., DH_M], mm[:, :N_HEADS_M, 0],
                       halo[:, 8 - (CONV_W - 1):, :]))
        qt_s, k_s, ckv_s, _, kr_s, sza_s, m_in_s, szm_s = _proj(ys, cos_s, sin_s, wp, nbd, nbd, nbd)
        ctx = _attn_sample(page_table, jnp.transpose(qt_s[0, :, 0], (2, 0, 1)), jnp.transpose(k_s[0], (1, 0, 2)),
                           ckv_s.reshape(nbd, 1, KV_LORA), wp, cache_ckv, cache_krope_t, l, pb, sub)
        m0 = jnp.concatenate([state_m[l], jnp.zeros((nbd, LANES - N_HEADS_M), f32)], axis=1)
        to_lead = lambda a: jnp.transpose(a, (1, 0, 2))
        y_s, c1, n1, m1, conv1 = _sample_tail(ys[0], to_lead(ctx), sza_s[0], m_in_s[0], szm_s[0],
                                              to_lead(state_conv[l]), state_C[l], to_lead(state_n[l]), m0, wp, bb)
        ys = y_s.reshape(1, nbd, D_MODEL)
        outs_s.append((ckv_s.reshape(nbd, 1, KV_LORA), kr_s.reshape(nbd, 1, QK_ROPE), c1, to_lead(n1),
                       m1[:, :N_HEADS_M], to_lead(conv1)))
    stk = lambda outs, i: jnp.stack([o[i] for o in outs], axis=0)
    return (yp, ys.reshape(nbd, 1, D_MODEL),
            stk(outs_p, 0), stk(outs_p, 1), stk(outs_p, 2), stk(outs_p, 3), stk(outs_p, 4), stk(outs_p, 5),
            stk(outs_s, 0), stk(outs_s, 1), stk(outs_s, 2), stk(outs_s, 3), stk(outs_s, 4), stk(outs_s, 5))
```

```python
import functools

import jax
import jax.numpy as jnp
from jax import lax
from jax.experimental import pallas as pl
from jax.experimental.pallas import tpu as pltpu

f32 = jnp.float32
bf16 = jnp.bfloat16

D_MODEL = 1024
D_ATT = 512
D_MLS = 512
N_HEADS_A = 4
QK_NOPE = 128
QK_ROPE = 64
V_HEAD = 128
Q_LORA = 384
KV_LORA = 256
ROPE_BASE = 10000.0
N_HEADS_M = 4
DH_M = 128
CONV_W = 4
PAGE_SIZE = 128
EPS = 1e-6
LN_EPS = 1e-5
ATT_SCALE = (QK_NOPE + QK_ROPE) ** -0.5
LOG2E = 1.4426950408889634
Q_SCALE = ATT_SCALE * LOG2E
K_SCALE_M = DH_M ** -0.5
SEQ_LAG = 2
AHEAD = 4
PROJ_ROWS = 128

LANES = 128
QK_PAD = 256
VMEM_LIMIT = 56 * 1024 * 1024

NT_DIMS = (((1,), (1,)), ((), ()))


def _rms(x, n):
    ms = jnp.sum(x * x, axis=-1, keepdims=True) * (1.0 / n)
    return x * lax.rsqrt(ms + EPS)


def _silu(x):
    return x * jax.nn.sigmoid(x)


def _dot(a, b):
    return jnp.dot(a, b, preferred_element_type=f32)


def _dot_nt(a, b):
    return lax.dot_general(a, b, NT_DIMS, preferred_element_type=f32)


def _rep_lanes(x, n):
    return x if n == 1 else jnp.concatenate([x] * n, axis=1)


def _rope_table_kernel(ang_ref, cos_ref, sin_ref):
    ang = ang_ref[...]
    lane = lax.broadcasted_iota(jnp.int32, ang.shape, 1)
    cos_ref[...] = jnp.cos(ang)
    s = jnp.sin(ang)
    sin_ref[...] = jnp.where((lane % QK_ROPE) < QK_ROPE // 2, -s, s)


def _rope_tables(pos):
    half = QK_ROPE // 2
    inv = 1.0 / (ROPE_BASE ** (jnp.arange(0, QK_ROPE, 2, dtype=f32) / QK_ROPE))
    ang = pos.astype(f32)[:, None] * inv[None, :]
    ang = jnp.tile(ang, (1, LANES // half))
    s = ang.shape[0]
    return pl.pallas_call(
        _rope_table_kernel,
        out_shape=(jax.ShapeDtypeStruct((s, LANES), f32), jax.ShapeDtypeStruct((s, LANES), f32)),
        name="rope_tables",
    )(ang)


def _proj_kernel(x_ref, cos_ref, sin_ref, ng_ref, w1_ref, w2_ref, qg_ref, wuq_ref, kvg_ref, wuk_ref,
                 gqn_ref, gqr_ref, gkn_ref, gkr_ref,
                 qt_ref, k_ref, ckv_ref, ct_ref, kr_ref, sza_ref, min_ref, szm_ref):
    tm = x_ref.shape[1]
    rows = min(tm, PROJ_ROWS)
    tq = qt_ref.shape[4]
    tk = ct_ref.shape[3]
    lane = lax.broadcasted_iota(jnp.int32, (rows, LANES), 1)
    first_half = (lane % QK_ROPE) < QK_ROPE // 2

    for r0 in range(0, tm, rows):
        rs = slice(r0, r0 + rows)
        x = x_ref[0, rs, :]
        h = (_rms(x, D_MODEL) * ng_ref[...]).astype(bf16)
        p1 = _dot(h, w1_ref[...])
        p2 = _dot(h, w2_ref[...])
        cos = cos_ref[rs, :]
        sin = sin_ref[rs, :]

        def rope(xp):
            sw = jnp.where(first_half, pltpu.roll(xp, LANES - QK_ROPE // 2, 1), pltpu.roll(xp, QK_ROPE // 2, 1))
            return xp * cos + sw * sin

        ql = (_rms(p1[:, :Q_LORA], Q_LORA) * qg_ref[...]).astype(bf16)
        qf = _dot(ql, wuq_ref[...])
        for hh in range(N_HEADS_A):
            qn = _rms(qf[:, LANES * hh:LANES * (hh + 1)], QK_NOPE) * gqn_ref[...]
            o = N_HEADS_A * QK_NOPE + LANES * hh
            qr = rope(_rms(qf[:, o:o + LANES], QK_ROPE) * gqr_ref[...])
            qh = jnp.concatenate([qn, qr], axis=1) * Q_SCALE
            qt_ref[0, hh, r0 // tq, :, r0 % tq:r0 % tq + rows] = qh.T.astype(bf16)

        c = _rms(p1[:, Q_LORA:Q_LORA + KV_LORA], KV_LORA) * kvg_ref[...]
        ckv_ref[0, rs, :] = c
        cb = c.astype(bf16)
        ct_ref[0, r0 // tk, :, r0 % tk:r0 % tk + rows] = c.T.astype(bf16)
        kn = _dot(cb, wuk_ref[...])
        o = Q_LORA + KV_LORA
        krp = rope(_rms(p1[:, o:o + LANES], QK_ROPE) * gkr_ref[...])
        kr_ref[0, rs, :] = krp[:, :QK_ROPE]
        krb = krp.astype(bf16)
        for hh in range(N_HEADS_A):
            knh = _rms(kn[:, LANES * hh:LANES * (hh + 1)], QK_NOPE) * gkn_ref[...]
            k_ref[0, hh, rs, 0:LANES] = knh.astype(bf16)
            k_ref[0, hh, rs, LANES:QK_PAD] = krb

        sza_ref[0, rs, :] = _silu(p2[:, :D_ATT]).astype(bf16)
        min_ref[0, rs, :] = p2[:, D_ATT:D_ATT + D_MLS]
        szm_ref[0, rs, :] = _silu(p2[:, D_ATT + D_MLS:]).astype(bf16)


def _proj(x, cos, sin, wp, tm, tq, tk):
    nb, s, _ = x.shape
    grid = (nb, s // tm)
    full = lambda a: pl.BlockSpec(a.shape, lambda b, i: (0,) * a.ndim)
    tok = lambda w: pl.BlockSpec((1, tm, w), lambda b, i: (b, i, 0))
    head = pl.BlockSpec((1, N_HEADS_A, tm, QK_PAD), lambda b, i: (b, 0, i, 0))
    head_t = pl.BlockSpec((1, N_HEADS_A, tm // tq, QK_PAD, tq), lambda b, i: (b, 0, i, 0, 0))
    lat_t = pl.BlockSpec((1, tm // tk, KV_LORA, tk), lambda b, i: (b, i, 0, 0))
    tab = pl.BlockSpec((tm, LANES), lambda b, i: (i, 0))
    weights = (wp["norm_g"], wp["w1"], wp["w2"], wp["q_norm_g"], wp["w_uq"], wp["kv_norm_g"], wp["w_uk"],
               wp["g_qn"], wp["g_qr"], wp["g_kn"], wp["g_kr"])
    out_shape = (
        jax.ShapeDtypeStruct((nb, N_HEADS_A, s // tq, QK_PAD, tq), bf16),
        jax.ShapeDtypeStruct((nb, N_HEADS_A, s, QK_PAD), bf16),
        jax.ShapeDtypeStruct((nb, s, KV_LORA), f32),
        jax.ShapeDtypeStruct((nb, s // tk, KV_LORA, tk), bf16),
        jax.ShapeDtypeStruct((nb, s, QK_ROPE), f32),
        jax.ShapeDtypeStruct((nb, s, D_ATT), bf16),
        jax.ShapeDtypeStruct((nb, s, D_MLS), f32),
        jax.ShapeDtypeStruct((nb, s, D_MLS), bf16),
    )
    out_specs = (head_t, head, tok(KV_LORA), lat_t, tok(QK_ROPE), tok(D_ATT), tok(D_MLS), tok(D_MLS))
    return pl.pallas_call(
        _proj_kernel,
        grid=grid,
        in_specs=[tok(D_MODEL), tab, tab] + [full(w) for w in weights],
        out_specs=out_specs,
        out_shape=out_shape,
        compiler_params=pltpu.CompilerParams(dimension_semantics=("arbitrary", "arbitrary"),
                                             vmem_limit_bytes=VMEM_LIMIT),
        name="in_proj",
    )(x, cos, sin, *weights)


def _attn_kernel(qt_ref, k_ref, ct_ref, sza_ref, wuv_ref, x_ref, mm_ref, woa_ref, wom_ref, y_ref,
                 m_sc, l_sc, acc_sc, s_sc, mixa_sc, *, tq, tk):
    i = pl.program_id(1)
    ratio = tq // tk
    kv_idx = lax.broadcasted_iota(jnp.int32, (tk, tq), 0)
    q_idx = lax.broadcasted_iota(jnp.int32, (tk, tq), 1)

    m_sc[...] = jnp.full(m_sc.shape, -jnp.inf, f32)
    l_sc[...] = jnp.zeros(l_sc.shape, f32)
    acc_sc[...] = jnp.zeros(acc_sc.shape, f32)

    def steps(tiles):
        units = [(t, hh) for t in range(len(tiles)) for hh in range(N_HEADS_A)]

        def scores(u):
            t, hh = units[u]
            j, diag = tiles[t]
            start = pl.multiple_of(j * tk, tk)
            s = _dot(k_ref[0, hh, pl.ds(start, tk), :], qt_ref[0, hh, 0])
            s_sc[t % 2, hh] = s if diag is None else jnp.where(kv_idx + diag * tk <= q_idx, s, -jnp.inf)

        def update(u):
            t, hh = units[u]
            s = s_sc[t % 2, hh]
            m_old = m_sc[hh]
            m_new = jnp.maximum(m_old, jnp.max(s, axis=0, keepdims=True))
            alpha = jnp.exp2(m_old - m_new)
            p = jnp.exp2(s - m_new[0:1, :])
            l_sc[hh] = alpha * l_sc[hh] + jnp.sum(p, axis=0, keepdims=True)
            acc_sc[hh] = alpha[0:1, :] * acc_sc[hh] + _dot(ct_ref[0, tiles[t][0]], p.astype(bf16))
            m_sc[hh] = m_new

        for u in range(min(AHEAD, len(units))):
            scores(u)
        for u in range(len(units)):
            if u + AHEAD < len(units):
                scores(u + AHEAD)
            update(u)

    def body(jj, carry):
        steps([(ratio * jj + d, None) for d in range(ratio)])
        return carry

    lax.fori_loop(0, i, body, 0)
    steps([(ratio * i + d, d) for d in range(ratio)])

    for hh in range(N_HEADS_A):
        inv = 1.0 / l_sc[hh]
        ctx = (acc_sc[hh] * inv[0:1, :]).T.astype(bf16)
        att = _dot(ctx, wuv_ref[hh])
        sl = slice(V_HEAD * hh, V_HEAD * (hh + 1))
        mixa_sc[:, sl] = (att * sza_ref[0, :, sl].astype(f32)).astype(bf16)
    y_ref[0] = x_ref[0] + _dot(mixa_sc[...], woa_ref[...]) + _dot(mm_ref[0], wom_ref[...])


def _attn_prompt(qt, k, ct, sza, x, mixm, wp, tq, tk):
    nb, _, s, _ = k.shape
    assert tq % tk == 0
    assert qt.shape == (nb, N_HEADS_A, s // tq, QK_PAD, tq) and ct.shape == (nb, s // tk, KV_LORA, tk)
    full = lambda a: pl.BlockSpec(a.shape, lambda b, i: (0,) * a.ndim)
    tok = lambda w: pl.BlockSpec((1, tq, w), lambda b, i: (b, i, 0))
    return pl.pallas_call(
        functools.partial(_attn_kernel, tq=tq, tk=tk),
        grid=(nb, s // tq),
        in_specs=[
            pl.BlockSpec((1, N_HEADS_A, 1, QK_PAD, tq), lambda b, i: (b, 0, i, 0, 0)),
            pl.BlockSpec((1, N_HEADS_A, s, QK_PAD), lambda b, i: (b, 0, 0, 0)),
            pl.BlockSpec((1, s // tk, KV_LORA, tk), lambda b, i: (b, 0, 0, 0)),
            tok(D_ATT), full(wp["w_uv"]), tok(D_MODEL), tok(D_MLS), full(wp["w_out_a"]), full(wp["w_out_m"]),
        ],
        out_specs=tok(D_MODEL),
        out_shape=jax.ShapeDtypeStruct((nb, s, D_MODEL), f32),
        scratch_shapes=[pltpu.VMEM((N_HEADS_A, 8, tq), f32),
                        pltpu.VMEM((N_HEADS_A, 8, tq), f32),
                        pltpu.VMEM((N_HEADS_A, KV_LORA, tq), f32),
                        pltpu.VMEM((2, N_HEADS_A, tk, tq), f32),
                        pltpu.VMEM((tq, D_ATT), bf16)],
        compiler_params=pltpu.CompilerParams(dimension_semantics=("arbitrary", "arbitrary"),
                                             vmem_limit_bytes=VMEM_LIMIT),
        name="attn_prompt",
    )(qt, k, ct, sza, wp["w_uv"], x, mixm, wp["w_out_a"], wp["w_out_m"])


def _scan_lanes(x, op, fill, length):
    lane = lax.broadcasted_iota(jnp.int32, x.shape, 1) % length
    d = 1
    while d < length:
        x = op(x, jnp.where(lane >= d, pltpu.roll(x, d, 1), fill))
        d *= 2
    return x


def _mlstm_kernel(min_ref, szm_ref, cw_ref, cb_ref, wqk_ref, wv_ref, wg_ref, gb_ref, lng_ref, skip_ref, sel_ref,
                  mix_ref, cn_ref, m_ref, halo_ref,
                  xbuf, ca_sc, qkv_sc, ks_sc, cn_sc, m_sc, cols_sc, inter_sc, sqk_sc, *, ts, lc, ne):
    si = pl.program_id(1)
    ns = pl.num_programs(1)
    nc = ts // lc

    @pl.when(si == 0)
    def _():
        xbuf[:, 0:8, :] = jnp.zeros((ne, 8, D_MLS), f32)
        cn_sc[...] = jnp.zeros(cn_sc.shape, f32)
        m_sc[...] = jnp.zeros(m_sc.shape, f32)

    row = lax.broadcasted_iota(jnp.int32, (lc, lc), 0)
    col = lax.broadcasted_iota(jnp.int32, (lc, lc), 1)
    causal = col <= row
    ones_b = jnp.ones((lc, LANES), bf16)

    def projections(e):
        x = min_ref[e]
        xbuf[e, 8:8 + ts, :] = x
        y = cb_ref[...] + xbuf[e, 8:8 + ts, :] * cw_ref[3:4, :]
        for j in range(CONV_W - 1):
            y = y + xbuf[e, 5 + j:5 + j + ts, :] * cw_ref[j:j + 1, :]
        xbuf[e, 0:8, :] = xbuf[e, ts:ts + 8, :]
        ca = _silu(y)
        ca_sc[e] = ca
        cab = ca.astype(bf16)
        xb = x.astype(bf16)
        for hh in range(N_HEADS_M):
            sl = slice(DH_M * hh, DH_M * (hh + 1))
            qk = _dot(cab[:, sl], wqk_ref[hh])
            qkv_sc[e, :, sl] = qk[:, :DH_M].astype(bf16)
            kh = qk[:, DH_M:]
            qkv_sc[e, :, D_MLS + DH_M * hh:D_MLS + DH_M * (hh + 1)] = kh.astype(bf16)
            ks_sc[e, :, sl] = (kh * K_SCALE_M).astype(bf16)
            qkv_sc[e, :, 2 * D_MLS + DH_M * hh:2 * D_MLS + DH_M * (hh + 1)] = (
                _dot(xb[:, sl], wv_ref[hh]).astype(bf16))

    def gate_scans(e):
        g = _dot(qkv_sc[e], wg_ref[...]) + gb_ref[...]
        gt = g.T[0:8, :]
        head_row = lax.broadcasted_iota(jnp.int32, gt.shape, 0) < N_HEADS_M
        li_all = jnp.where(head_row, gt, 0.0)
        lf_all = jnp.where(head_row, jax.nn.log_sigmoid(pltpu.roll(gt, 4, 0)), 0.0)
        for c in range(nc):
            for hh in range(N_HEADS_M):
                cs = slice(lc * c, lc * (c + 1))
                sl = slice(DH_M * hh, DH_M * (hh + 1))
                sqk_sc[e, c, hh] = _dot_nt(qkv_sc[e, cs, sl], ks_sc[e, cs, sl])
        b_all = _scan_lanes(lf_all, jnp.add, 0.0, lc)
        u_all = li_all - b_all
        cm_all = _scan_lanes(u_all, jnp.maximum, -jnp.inf, lc)
        u_max = [jnp.max(u_all[:, lc * c:lc * (c + 1)], axis=1, keepdims=True) for c in range(nc)]
        lf_sum = [jnp.sum(lf_all[:, lc * c:lc * (c + 1)], axis=1, keepdims=True) for c in range(nc)]
        return b_all, u_all, cm_all, u_max, lf_sum

    def chunk_refs(e, c, hh):
        cs = slice(lc * c, lc * (c + 1))
        sl = slice(DH_M * hh, DH_M * (hh + 1))
        vc = qkv_sc[e, cs, 2 * D_MLS + DH_M * hh:2 * D_MLS + DH_M * (hh + 1)]
        v_aug = jnp.concatenate([vc, ones_b], axis=1)
        return cs, sl, qkv_sc[e, cs, sl], ks_sc[e, cs, sl], v_aug

    def state_pass(e, c, scans, m0):
        b_all, u_all, cm_all, u_max, lf_sum = scans
        cs = slice(lc * c, lc * (c + 1))
        u = u_all[:, cs]
        gmax = jnp.maximum(m0, cm_all[:, cs])
        mt = b_all[:, cs] + gmax
        g_last = jnp.maximum(m0, u_max[c])
        m_new = lf_sum[c] + g_last
        decay = jnp.exp(m0 - g_last)
        w_s = jnp.exp(u - g_last)
        r = jnp.concatenate([gmax, mt], axis=0)
        r_hi = r.astype(bf16).astype(f32)
        r_mid = (r - r_hi).astype(bf16).astype(f32)
        r_lo = r - r_hi - r_mid
        stack = jnp.concatenate([r_hi, r_mid, r_lo, jnp.zeros((lc - 48, lc), f32)], axis=0)
        cols_sc[e, c] = _dot(stack.T.astype(bf16), sel_ref[...])
        for hh in range(N_HEADS_M):
            _, _, qc, kc, v_aug = chunk_refs(e, c, hh)
            inter_sc[e, c, hh] = _dot(qc, cn_sc[e, hh].astype(bf16))
            ktw = (kc.astype(f32).T * w_s[hh:hh + 1, :]).astype(bf16)
            upd = _dot(ktw, v_aug)
            dec = jnp.broadcast_to(decay[hh:hh + 1, :], (DH_M, LANES))
            cn_sc[e, hh] = _rep_lanes(dec, 2) * cn_sc[e, hh] + upd
        return u, m_new

    def output_pass(e, c, u, m0):
        for hh in range(N_HEADS_M):
            cs, sl, _, _, v_aug = chunk_refs(e, c, hh)
            g_col = cols_sc[e, c, :, LANES * hh:LANES * (hh + 1)]
            mt_col = cols_sc[e, c, :, LANES * (N_HEADS_M + hh):LANES * (N_HEADS_M + hh + 1)]
            wm = jnp.where(causal, jnp.exp(u[hh:hh + 1, :] - g_col), 0.0)
            p = (sqk_sc[e, c, hh] * wm).astype(bf16)
            intra = _dot(p, v_aug)
            w_inter = jnp.exp(m0[hh:hh + 1, :] - g_col)
            tot = _rep_lanes(w_inter, 2) * inter_sc[e, c, hh] + intra
            num = tot[:, :DH_M]
            den = tot[:, DH_M:]
            hv = num / jnp.maximum(jnp.abs(den), jnp.exp(-mt_col))
            mu = jnp.mean(hv, axis=-1, keepdims=True)
            hc = hv - mu
            var = jnp.mean(hc * hc, axis=-1, keepdims=True)
            hn = hc * lax.rsqrt(var + LN_EPS) * lng_ref[:, sl]
            out = hn + skip_ref[:, sl] * ca_sc[e, cs, sl]
            mix_ref[e, cs, sl] = (out * szm_ref[e, cs, sl].astype(f32)).astype(bf16)

    seqs = range(ne)
    scans = []
    for e in seqs:
        projections(e)
        scans.append(gate_scans(e))
    m_cur = [m_sc[e] for e in seqs]
    staged = [[] for _ in seqs]
    for t in range(nc + 1 + SEQ_LAG * (ne - 1)):
        for e in seqs:
            c = t - SEQ_LAG * e
            if 0 <= c < nc:
                u_c, m_next = state_pass(e, c, scans[e], m_cur[e])
                staged[e].append((u_c, m_cur[e]))
                m_cur[e] = m_next
            if 1 <= c <= nc:
                output_pass(e, c - 1, *staged[e][c - 1])
    for e in seqs:
        m_sc[e] = m_cur[e]

    @pl.when(si == ns - 1)
    def _():
        cn_ref[...] = cn_sc[...]
        m_ref[...] = m_sc[...]
        halo_ref[...] = xbuf[:, 0:8, :]


def _column_selector():
    rows = lax.broadcasted_iota(jnp.int32, (LANES, 2 * N_HEADS_M * LANES), 0)
    block = lax.broadcasted_iota(jnp.int32, (LANES, 2 * N_HEADS_M * LANES), 1) // LANES
    src = 8 * (block // N_HEADS_M) + block % N_HEADS_M
    return ((rows < 48) & (rows % 16 == src)).astype(bf16)


def _mlstm_prompt(m_in, szm, wp, ts, lc):
    nb, s, _ = m_in.shape
    full = lambda a: pl.BlockSpec(a.shape, lambda b, i: (0,) * a.ndim)
    ne = 2 if nb % 2 == 0 else 1
    tok = pl.BlockSpec((ne, ts, D_MLS), lambda b, i: (b, i, 0))
    assert lc == LANES, "one chunk spans exactly one vreg of lanes"
    weights = (wp["conv_w"], wp["conv_b"], wp["w_qk_m"], wp["w_v_m"], wp["w_gate"], wp["gate_b"],
               wp["mh_norm_g"], wp["skip"], _column_selector())
    nc = ts // lc
    return pl.pallas_call(
        functools.partial(_mlstm_kernel, ts=ts, lc=lc, ne=ne),
        grid=(nb // ne, s // ts),
        in_specs=[tok, tok] + [full(w) for w in weights],
        out_specs=(tok,
                   pl.BlockSpec((ne, N_HEADS_M, DH_M, 2 * DH_M), lambda b, i: (b, 0, 0, 0)),
                   pl.BlockSpec((ne, 8, LANES), lambda b, i: (b, 0, 0)),
                   pl.BlockSpec((ne, 8, D_MLS), lambda b, i: (b, 0, 0))),
        out_shape=(jax.ShapeDtypeStruct((nb, s, D_MLS), bf16),
                   jax.ShapeDtypeStruct((nb, N_HEADS_M, DH_M, 2 * DH_M), f32),
                   jax.ShapeDtypeStruct((nb, 8, LANES), f32),
                   jax.ShapeDtypeStruct((nb, 8, D_MLS), f32)),
        scratch_shapes=[pltpu.VMEM((ne, ts + 8, D_MLS), f32),
                        pltpu.VMEM((ne, ts, D_MLS), f32),
                        pltpu.VMEM((ne, ts, 3 * D_MLS), bf16),
                        pltpu.VMEM((ne, ts, D_MLS), bf16),
                        pltpu.VMEM((ne, N_HEADS_M, DH_M, 2 * DH_M), f32),
                        pltpu.VMEM((ne, 8, LANES), f32),
                        pltpu.VMEM((ne, nc, lc, 2 * N_HEADS_M * LANES), f32),
                        pltpu.VMEM((ne, nc, N_HEADS_M, lc, 2 * DH_M), f32),
                        pltpu.VMEM((ne, nc, N_HEADS_M, lc, lc), f32)],
        compiler_params=pltpu.CompilerParams(dimension_semantics=("arbitrary", "arbitrary"),
                                             vmem_limit_bytes=VMEM_LIMIT),
        name="mlstm_prompt",
    )(m_in, szm, *weights)


def _sattn_kernel(pt_ref, q_ref, knew_ref, cnew_ref, wukt_ref, gkn_ref, ckv_hbm, krt_hbm, ctx_ref,
                  cbuf0, cbuf1, kbuf0, kbuf1, sem, cb_sc, knt_sc, lhs_sc, *, layer, pb, sub):
    b = pl.program_id(0)
    nbatch = pl.num_programs(0)
    npages = pt_ref.shape[1]
    nblk = npages // pb
    tb = pb * PAGE_SIZE
    nsub = tb // sub
    sub_pages = sub // PAGE_SIZE
    cbufs = (cbuf0, cbuf1)
    kbufs = (kbuf0, kbuf1)

    def part_copies(bb, blk, slot, si):
        latent, ropek = [], []
        for p in range(sub_pages * si, sub_pages * (si + 1)):
            page = pt_ref[bb, blk * pb + p]
            rows = pl.ds(p * PAGE_SIZE, PAGE_SIZE)
            latent.append(pltpu.make_async_copy(ckv_hbm.at[layer, page], cbufs[slot].at[rows], sem.at[slot, si, 0]))
            ropek.append(pltpu.make_async_copy(krt_hbm.at[layer, page], kbufs[slot].at[p], sem.at[slot, si, 1]))
        return latent + ropek

    def start_part(bb, blk, slot, si):
        for cp in part_copies(bb, blk, slot, si):
            cp.start()

    def wait_part(bb, blk, slot, si):
        for cp in part_copies(bb, blk, slot, si):
            cp.wait()

    def two_ahead(blk):
        wraps = blk + 2 >= nblk
        return jnp.where(wraps, jnp.minimum(b + 1, nbatch - 1), b), jnp.where(wraps, blk + 2 - nblk, blk + 2)

    @pl.when(b == 0)
    def _():
        for slot in range(2):
            for si in range(nsub):
                start_part(0, slot, slot, si)

    q = q_ref[0]
    qg = q[:, :QK_NOPE].astype(f32) * gkn_ref[...]
    rowq = lax.broadcasted_iota(jnp.int32, (16, QK_NOPE), 0)
    qabs = jnp.zeros((16, KV_LORA), f32)
    for hh in range(N_HEADS_A):
        lhs = jnp.where(rowq == hh, jnp.broadcast_to(qg[hh:hh + 1, :], (16, QK_NOPE)), 0.0).astype(bf16)
        qabs = qabs + _dot(lhs, wukt_ref[QK_NOPE * hh:QK_NOPE * (hh + 1), :])
    nk = N_HEADS_A * QK_NOPE
    lhs_sc[0:nk, :] = wukt_ref[...]
    lhs_sc[nk:nk + 16, :] = qabs.astype(bf16)
    qr16 = jnp.concatenate([q[:, QK_NOPE:QK_NOPE + QK_ROPE],
                            jnp.zeros((16 - N_HEADS_A, QK_ROPE), bf16)], axis=0)

    row16 = lax.broadcasted_iota(jnp.int32, (16, sub), 0)

    def matmuls(slot, si):
        cb = cbufs[slot][sub * si:sub * (si + 1), :].astype(bf16)
        cb_sc[si % 2] = cb
        knt_sc[si % 2] = _dot_nt(lhs_sc[...], cb)
        srope = jnp.concatenate(
            [_dot(qr16, kbufs[slot][sub_pages * si + p].astype(bf16)) for p in range(sub_pages)],
            axis=1)
        return srope

    def softmax_update(si, srope, carry):
        num = knt_sc[si % 2, nk:nk + 16, :]
        m, l, acc = carry
        rnorm = jnp.zeros((16, sub), f32)
        for hh in range(N_HEADS_A):
            kh = knt_sc[si % 2, QK_NOPE * hh:QK_NOPE * (hh + 1), :]
            ss = jnp.sum(kh * kh, axis=0, keepdims=True)
            rnorm = jnp.where(row16 == hh, lax.rsqrt(ss * (1.0 / QK_NOPE) + EPS), rnorm)
        s = num * rnorm + srope
        m_new = jnp.maximum(m, jnp.max(s, axis=1, keepdims=True))
        alpha = jnp.exp2(m - m_new)
        p = jnp.exp2(s - _rep_lanes(m_new, sub // LANES))
        l = alpha * l + jnp.sum(p, axis=1, keepdims=True)
        acc = _rep_lanes(alpha, KV_LORA // LANES) * acc + _dot(p.astype(bf16), cb_sc[si % 2])
        return m_new, l, acc

    def compute(slot, blk, carry):
        nxt_b, nxt_blk = two_ahead(blk)
        for si in range(nsub):
            wait_part(b, blk, slot, si)

        def stage(si):
            staged = matmuls(slot, si)
            start_part(nxt_b, nxt_blk, slot, si)
            return staged

        staged = stage(0)
        for si in range(nsub):
            nxt = stage(si + 1) if si + 1 < nsub else None
            carry = softmax_update(si, staged, carry)
            staged = nxt
        return carry

    def pair(i, carry):
        return compute(1, 2 * i + 1, compute(0, 2 * i, carry))

    init = (jnp.full((16, LANES), -jnp.inf, f32), jnp.zeros((16, LANES), f32), jnp.zeros((16, KV_LORA), f32))
    m_old, l_old, acc_old = lax.fori_loop(0, nblk // 2, pair, init)

    @pl.when(b == nbatch - 1)
    def _():
        for slot in range(2):
            for si in range(nsub):
                wait_part(b, slot, slot, si)

    kn = knew_ref[0].astype(f32)
    s_new = jnp.sum(q.astype(f32) * kn, axis=1, keepdims=True)
    s_new = jnp.concatenate([jnp.broadcast_to(s_new, (N_HEADS_A, LANES)),
                             jnp.zeros((16 - N_HEADS_A, LANES), f32)], axis=0)
    m_new = jnp.maximum(m_old, s_new)
    alpha = jnp.exp2(m_old - m_new)
    p_new = jnp.exp2(s_new - m_new)
    l = alpha * l_old + p_new
    acc = _rep_lanes(alpha, 2) * acc_old + _rep_lanes(p_new, 2) * cnew_ref[0]
    ctx = acc * _rep_lanes(1.0 / l, 2)
    ctx_ref[0] = ctx[0:N_HEADS_A, :]


def _attn_sample(page_table, q_s, k_s, c_s, wp, cache_ckv, cache_krope_t, layer, pb, sub):
    nbd = q_s.shape[0]
    tb = pb * PAGE_SIZE
    full = lambda a: pl.BlockSpec(a.shape, lambda b, pt: (0,) * a.ndim)
    grid_spec = pltpu.PrefetchScalarGridSpec(
        num_scalar_prefetch=1,
        grid=(nbd,),
        in_specs=[pl.BlockSpec((1, N_HEADS_A, QK_PAD), lambda b, pt: (b, 0, 0)),
                  pl.BlockSpec((1, N_HEADS_A, QK_PAD), lambda b, pt: (b, 0, 0)),
                  pl.BlockSpec((1, 1, KV_LORA), lambda b, pt: (b, 0, 0)),
                  full(wp["w_uk_t"]), full(wp["g_kn"]),
                  pl.BlockSpec(memory_space=pl.ANY),
                  pl.BlockSpec(memory_space=pl.ANY)],
        out_specs=pl.BlockSpec((1, N_HEADS_A, KV_LORA), lambda b, pt: (b, 0, 0)),
        scratch_shapes=[pltpu.VMEM((tb, KV_LORA), f32),
                        pltpu.VMEM((tb, KV_LORA), f32),
                        pltpu.VMEM((pb, QK_ROPE, PAGE_SIZE), f32),
                        pltpu.VMEM((pb, QK_ROPE, PAGE_SIZE), f32),
                        pltpu.SemaphoreType.DMA((2, tb // sub, 2)),
                        pltpu.VMEM((2, sub, KV_LORA), bf16),
                        pltpu.VMEM((2, N_HEADS_A * QK_NOPE + 16, sub), f32),
                        pltpu.VMEM((N_HEADS_A * QK_NOPE + 16, KV_LORA), bf16)],
    )
    return pl.pallas_call(
        functools.partial(_sattn_kernel, layer=layer, pb=pb, sub=sub),
        grid_spec=grid_spec,
        out_shape=jax.ShapeDtypeStruct((nbd, N_HEADS_A, KV_LORA), f32),
        compiler_params=pltpu.CompilerParams(dimension_semantics=("arbitrary",), vmem_limit_bytes=VMEM_LIMIT),
        name="attn_sample",
    )(page_table, q_s, k_s, c_s, wp["w_uk_t"], wp["g_kn"], cache_ckv, cache_krope_t)


def _stail_kernel(x_ref, ctx_ref, sza_ref, min_ref, szm_ref, conv_ref, c0_ref, n0_ref, m0_ref,
                  wuv_ref, cw_ref, cb_ref, wqk_ref, wv_ref, wg_ref, gb_ref, lng_ref, skip_ref, woa_ref, wom_ref,
                  y_ref, c1_ref, n1_ref, m1_ref, conv1_ref, *, bb):
    mixa = []
    for hh in range(N_HEADS_A):
        att = _dot(ctx_ref[hh].astype(bf16), wuv_ref[hh])
        sl = slice(V_HEAD * hh, V_HEAD * (hh + 1))
        mixa.append((att * sza_ref[:, sl].astype(f32)).astype(bf16))
    mixa = jnp.concatenate(mixa, axis=1)

    x = min_ref[...]
    y = cb_ref[...] + x * cw_ref[3:4, :]
    for j in range(CONV_W - 1):
        y = y + conv_ref[j] * cw_ref[j:j + 1, :]
    for j in range(CONV_W - 2):
        conv1_ref[j] = conv_ref[j + 1]
    conv1_ref[CONV_W - 2] = x
    ca = _silu(y)
    cab = ca.astype(bf16)
    xb = x.astype(bf16)
    qs, ks, vs = [], [], []
    for hh in range(N_HEADS_M):
        sl = slice(DH_M * hh, DH_M * (hh + 1))
        qk = _dot(cab[:, sl], wqk_ref[hh])
        qs.append(qk[:, :DH_M])
        ks.append(qk[:, DH_M:])
        vs.append(_dot(xb[:, sl], wv_ref[hh]))
    qkv = jnp.concatenate(qs + ks + vs, axis=1).astype(bf16)
    g = _dot(qkv, wg_ref[...]) + gb_ref[...]
    li = g
    lf = jax.nn.log_sigmoid(pltpu.roll(g, LANES - N_HEADS_M, 1))
    m0 = m0_ref[...]
    a = lf + m0
    mt = jnp.maximum(a, li)
    w_inter = jnp.exp(a - mt)
    w_new = jnp.exp(li - mt)
    emt = jnp.exp(-mt)
    m1_ref[...] = mt

    rowi = lax.broadcasted_iota(jnp.int32, (bb, DH_M), 0)
    rowp = lax.broadcasted_iota(jnp.int32, (LANES, DH_M), 0)
    zpad = jnp.zeros((LANES - bb, DH_M), f32)
    mixm = []
    for hh in range(N_HEADS_M):
        sl = slice(DH_M * hh, DH_M * (hh + 1))
        col = lambda z: jnp.broadcast_to(z[:, hh:hh + 1], (bb, DH_M))
        wi, wn, em = col(w_inter), col(w_new), col(emt)
        qh = qs[hh]
        kh = ks[hh] * K_SCALE_M
        vh = vs[hh]
        qb_ = qh.astype(bf16)
        kw = (kh * wn)
        kwt = jnp.concatenate([kw, zpad], axis=0).T.astype(bf16)
        vb_ = vh.astype(bf16)
        vpad = jnp.concatenate([vh, zpad], axis=0)
        inter = jnp.zeros((bb, DH_M), f32)
        for r in range(bb):
            c0 = c0_ref[r, hh]
            inter = inter + jnp.where(rowi == r, _dot(qb_, c0.astype(bf16)), 0.0)
            upd = _dot(kwt, jnp.where(rowp == r, vpad, 0.0).astype(bf16))
            dec = jnp.broadcast_to(wi[r:r + 1, :], (DH_M, DH_M))
            c1_ref[r, hh] = dec * c0 + upd
        n0 = n0_ref[hh]
        n1_ref[hh] = wi * n0 + kw
        qk_dot = jnp.sum(qb_.astype(f32) * kh.astype(bf16).astype(f32), axis=1, keepdims=True) * wn
        num = wi * inter + qk_dot * vb_.astype(f32)
        den = wi * jnp.sum(qb_.astype(f32) * n0.astype(bf16).astype(f32), axis=1, keepdims=True) + qk_dot
        hv = num / jnp.maximum(jnp.abs(den), em)
        mu = jnp.mean(hv, axis=-1, keepdims=True)
        hc = hv - mu
        var = jnp.mean(hc * hc, axis=-1, keepdims=True)
        hn = hc * lax.rsqrt(var + LN_EPS) * lng_ref[:, sl]
        out = hn + skip_ref[:, sl] * ca[:, sl]
        mixm.append((out * szm_ref[:, sl].astype(f32)).astype(bf16))
    mixm = jnp.concatenate(mixm, axis=1)
    y_ref[...] = x_ref[...] + _dot(mixa, woa_ref[...]) + _dot(mixm, wom_ref[...])


def _sample_tail(x_s, ctx, sza, m_in, szm, conv0, c0, n0, m0, wp, bb):
    nbd = x_s.shape[0]
    full = lambda a: pl.BlockSpec(a.shape, lambda i: (0,) * a.ndim)
    row = lambda w: pl.BlockSpec((bb, w), lambda i: (i, 0))
    weights = (wp["w_uv"], wp["conv_w"], wp["conv_b"], wp["w_qk_m"], wp["w_v_m"], wp["w_gate"], wp["gate_b"],
               wp["mh_norm_g"], wp["skip"], wp["w_out_a"], wp["w_out_m"])
    conv_spec = pl.BlockSpec((CONV_W - 1, bb, D_MLS), lambda i: (0, i, 0))
    c_spec = pl.BlockSpec((bb, N_HEADS_M, DH_M, DH_M), lambda i: (i, 0, 0, 0))
    n_spec = pl.BlockSpec((N_HEADS_M, bb, DH_M), lambda i: (0, i, 0))
    return pl.pallas_call(
        functools.partial(_stail_kernel, bb=bb),
        grid=(nbd // bb,),
        in_specs=[row(D_MODEL),
                  pl.BlockSpec((N_HEADS_A, bb, KV_LORA), lambda i: (0, i, 0)),
                  row(D_ATT), row(D_MLS), row(D_MLS), conv_spec, c_spec, n_spec, row(LANES)]
                 + [full(w) for w in weights],
        out_specs=(row(D_MODEL), c_spec, n_spec, row(LANES), conv_spec),
        out_shape=(jax.ShapeDtypeStruct((nbd, D_MODEL), f32),
                   jax.ShapeDtypeStruct(c0.shape, f32),
                   jax.ShapeDtypeStruct(n0.shape, f32),
                   jax.ShapeDtypeStruct((nbd, LANES), f32),
                   jax.ShapeDtypeStruct(conv0.shape, f32)),
        compiler_params=pltpu.CompilerParams(dimension_semantics=("arbitrary",), vmem_limit_bytes=VMEM_LIMIT),
        name="sample_tail",
    )(x_s, ctx, sza, m_in, szm, conv0, c0, n0, m0, *weights)


def _prep_weights(l, norm_g, w_in, q_norm_g, w_uq, kv_norm_g, w_uk, w_uv, g_qn, g_qr, g_kn, g_kr,
                  conv_w, conv_b, w_q_m, w_k_m, w_v_m, w_gate, b_i, b_f, mh_norm_g, skip, w_out):
    wi = w_in[l]
    o1 = Q_LORA + KV_LORA + QK_ROPE
    w1 = jnp.concatenate([wi[:, :o1], jnp.zeros((D_MODEL, LANES - QK_ROPE), f32)], axis=1)
    w2 = wi[:, o1:]
    wq = w_uq[l].reshape(Q_LORA, N_HEADS_A, QK_NOPE + QK_ROPE)
    wq_rope = jnp.concatenate([wq[:, :, QK_NOPE:], jnp.zeros((Q_LORA, N_HEADS_A, LANES - QK_ROPE), f32)], axis=2)
    wuq = jnp.concatenate([wq[:, :, :QK_NOPE].reshape(Q_LORA, -1), wq_rope.reshape(Q_LORA, -1)], axis=1)
    pad_rope = lambda g: jnp.concatenate([g, jnp.zeros((LANES - QK_ROPE,), f32)])[None, :]
    wuk = w_uk[l].reshape(KV_LORA, N_HEADS_A * QK_NOPE)
    wg = jnp.concatenate([w_gate[l], jnp.zeros((3 * D_MLS, LANES - 2 * N_HEADS_M), f32)], axis=1)
    gate_b = jnp.concatenate([b_i[l], b_f[l], jnp.zeros((LANES - 2 * N_HEADS_M,), f32)])[None, :]
    return {
        "norm_g": norm_g[l][None, :],
        "w1": w1.astype(bf16),
        "w2": w2.astype(bf16),
        "q_norm_g": q_norm_g[l][None, :],
        "w_uq": wuq.astype(bf16),
        "kv_norm_g": kv_norm_g[l][None, :],
        "w_uk": wuk.astype(bf16),
        "w_uk_t": wuk.T.astype(bf16),
        "w_uv": jnp.transpose(w_uv[l], (1, 0, 2)).astype(bf16),
        "g_qn": g_qn[l][None, :],
        "g_qr": pad_rope(g_qr[l]),
        "g_kn": g_kn[l][None, :],
        "g_kr": pad_rope(g_kr[l]),
        "conv_w": conv_w[l],
        "conv_b": conv_b[l][None, :],
        "w_qk_m": jnp.concatenate([w_q_m[l], w_k_m[l]], axis=2).astype(bf16),
        "w_v_m": w_v_m[l].astype(bf16),
        "w_gate": wg.astype(bf16),
        "gate_b": gate_b,
        "mh_norm_g": mh_norm_g[l][None, :],
        "skip": skip[l][None, :],
        "w_out_a": w_out[l][:D_ATT].astype(bf16),
        "w_out_m": w_out[l][D_ATT:].astype(bf16),
    }


def _pick(n, candidates):
    for c in candidates:
        if n % c == 0:
            return c
    raise ValueError(f"no tile size for extent {n}")


def kernel(x_prompt, x_sample, cache_ckv, cache_krope, state_C, state_n, state_m, state_conv, page_table,
           norm_g, w_in, q_norm_g, w_uq, kv_norm_g, w_uk, w_uv, g_qn, g_qr, g_kn, g_kr,
           conv_w, conv_b, w_q_m, w_k_m, w_v_m, w_gate, b_i, b_f, mh_norm_g, skip, w_out):
    nb, s, _ = x_prompt.shape
    nbd, sd, _ = x_sample.shape
    depth = norm_g.shape[0]
    assert sd == 1, "sample path handles one new token per sequence"
    npages = page_table.shape[1]
    past_len = npages * PAGE_SIZE

    cos_p, sin_p = _rope_tables(jnp.arange(s))
    cos_s, sin_s = _rope_tables(jnp.full((nbd,), past_len, jnp.int32))

    tm_p = _pick(s, (1024, 512, 256, 128))
    tk_att = _pick(s, (256, 128))
    tq_att = _pick(tm_p, (2 * tk_att, tk_att))
    ts_m = _pick(s, (512, 256, 128))
    assert npages % 2 == 0, "sample attention double-buffers an even number of page blocks"
    pb = _pick(npages // 2, (32, 16, 8, 4, 2, 1))
    sub = _pick(pb * PAGE_SIZE, (2048, 1024, 512, 256, 128))
    bb = _pick(nbd, (16,))

    cache_krope_t = jnp.swapaxes(cache_krope, 2, 3)

    yp = x_prompt
    ys = x_sample.reshape(1, nbd, D_MODEL)
    outs_p, outs_s = [], []
    for l in range(depth):
        wp = _prep_weights(l, norm_g, w_in, q_norm_g, w_uq, kv_norm_g, w_uk, w_uv, g_qn, g_qr, g_kn, g_kr,
                           conv_w, conv_b, w_q_m, w_k_m, w_v_m, w_gate, b_i, b_f, mh_norm_g, skip, w_out)
        qt, k, ckv, ct, kr, sza, m_in, szm = _proj(yp, cos_p, sin_p, wp, tm_p, tq_att, tk_att)
        mixm, cn, mm, halo = _mlstm_prompt(m_in, szm, wp, ts_m, LANES)
        yp = _attn_prompt(qt, k, ct, sza, yp, mixm, wp, tq_att, tk_att)
        outs_p.append((ckv, kr, cn[..., :DH_M], cn[..., DH_M], mm[:, :N_HEADS_M, 0],
                       halo[:, 8 - (CONV_W - 1):, :]))
        qt_s, k_s, ckv_s, _, kr_s, sza_s, m_in_s, szm_s = _proj(ys, cos_s, sin_s, wp, nbd, nbd, nbd)
        ctx = _attn_sample(page_table, jnp.transpose(qt_s[0, :, 0], (2, 0, 1)), jnp.transpose(k_s[0], (1, 0, 2)),
                           ckv_s.reshape(nbd, 1, KV_LORA), wp, cache_ckv, cache_krope_t, l, pb, sub)
        m0 = jnp.concatenate([state_m[l], jnp.zeros((nbd, LANES - N_HEADS_M), f32)], axis=1)
        to_lead = lambda a: jnp.transpose(a, (1, 0, 2))
        y_s, c1, n1, m1, conv1 = _sample_tail(ys[0], to_lead(ctx), sza_s[0], m_in_s[0], szm_s[0],
                                              to_lead(state_conv[l]), state_C[l], to_lead(state_n[l]), m0, wp, bb)
        ys = y_s.reshape(1, nbd, D_MODEL)
        outs_s.append((ckv_s.reshape(nbd, 1, KV_LORA), kr_s.reshape(nbd, 1, QK_ROPE), c1, to_lead(n1),
                       m1[:, :N_HEADS_M], to_lead(conv1)))
    stk = lambda outs, i: jnp.stack([o[i] for o in outs], axis=0)
    return (yp, ys.reshape(nbd, 1, D_MODEL),
            stk(outs_p, 0), stk(outs_p, 1), stk(outs_p, 2), stk(outs_p, 3), stk(outs_p, 4), stk(outs_p, 5),
            stk(outs_s, 0), stk(outs_s, 1), stk(outs_s, 2), stk(outs_s, 3), stk(outs_s, 4), stk(outs_s, 5))
```

```python
import functools

import jax
import jax.numpy as jnp
from jax import lax
from jax.experimental import pallas as pl
from jax.experimental.pallas import tpu as pltpu

f32 = jnp.float32
bf16 = jnp.bfloat16

D_MODEL = 1024
D_ATT = 512
D_MLS = 512
N_HEADS_A = 4
QK_NOPE = 128
QK_ROPE = 64
V_HEAD = 128
Q_LORA = 384
KV_LORA = 256
ROPE_BASE = 10000.0
N_HEADS_M = 4
DH_M = 128
CONV_W = 4
PAGE_SIZE = 128
EPS = 1e-6
LN_EPS = 1e-5
ATT_SCALE = (QK_NOPE + QK_ROPE) ** -0.5
LOG2E = 1.4426950408889634
Q_SCALE = ATT_SCALE * LOG2E
K_SCALE_M = DH_M ** -0.5
SEQ_LAG = 2
AHEAD = 4
PROJ_ROWS = 128

LANES = 128
QK_PAD = 256
VMEM_LIMIT = 56 * 1024 * 1024

NT_DIMS = (((1,), (1,)), ((), ()))


def _rms(x, n):
    ms = jnp.sum(x * x, axis=-1, keepdims=True) * (1.0 / n)
    return x * lax.rsqrt(ms + EPS)


def _silu(x):
    return x * jax.nn.sigmoid(x)


def _dot(a, b):
    return jnp.dot(a, b, preferred_element_type=f32)


def _dot_nt(a, b):
    return lax.dot_general(a, b, NT_DIMS, preferred_element_type=f32)


def _rep_lanes(x, n):
    return x if n == 1 else jnp.concatenate([x] * n, axis=1)


def _rope_table_kernel(ang_ref, cos_ref, sin_ref):
    ang = ang_ref[...]
    lane = lax.broadcasted_iota(jnp.int32, ang.shape, 1)
    cos_ref[...] = jnp.cos(ang)
    s = jnp.sin(ang)
    sin_ref[...] = jnp.where((lane % QK_ROPE) < QK_ROPE // 2, -s, s)


def _rope_tables(pos):
    half = QK_ROPE // 2
    inv = 1.0 / (ROPE_BASE ** (jnp.arange(0, QK_ROPE, 2, dtype=f32) / QK_ROPE))
    ang = pos.astype(f32)[:, None] * inv[None, :]
    ang = jnp.tile(ang, (1, LANES // half))
    s = ang.shape[0]
    return pl.pallas_call(
        _rope_table_kernel,
        out_shape=(jax.ShapeDtypeStruct((s, LANES), f32), jax.ShapeDtypeStruct((s, LANES), f32)),
        name="rope_tables",
    )(ang)


def _proj_kernel(x_ref, cos_ref, sin_ref, ng_ref, w1_ref, w2_ref, qg_ref, wuq_ref, kvg_ref, wuk_ref,
                 gqn_ref, gqr_ref, gkn_ref, gkr_ref,
                 qt_ref, k_ref, ckv_ref, ct_ref, kr_ref, sza_ref, min_ref, szm_ref):
    tm = x_ref.shape[1]
    rows = min(tm, PROJ_ROWS)
    tq = qt_ref.shape[4]
    tk = ct_ref.shape[3]
    lane = lax.broadcasted_iota(jnp.int32, (rows, LANES), 1)
    first_half = (lane % QK_ROPE) < QK_ROPE // 2

    for r0 in range(0, tm, rows):
        rs = slice(r0, r0 + rows)
        x = x_ref[0, rs, :]
        h = (_rms(x, D_MODEL) * ng_ref[...]).astype(bf16)
        p1 = _dot(h, w1_ref[...])
        p2 = _dot(h, w2_ref[...])
        cos = cos_ref[rs, :]
        sin = sin_ref[rs, :]

        def rope(xp):
            sw = jnp.where(first_half, pltpu.roll(xp, LANES - QK_ROPE // 2, 1), pltpu.roll(xp, QK_ROPE // 2, 1))
            return xp * cos + sw * sin

        ql = (_rms(p1[:, :Q_LORA], Q_LORA) * qg_ref[...]).astype(bf16)
        qf = _dot(ql, wuq_ref[...])
        for hh in range(N_HEADS_A):
            qn = _rms(qf[:, LANES * hh:LANES * (hh + 1)], QK_NOPE) * gqn_ref[...]
            o = N_HEADS_A * QK_NOPE + LANES * hh
            qr = rope(_rms(qf[:, o:o + LANES], QK_ROPE) * gqr_ref[...])
            qh = jnp.concatenate([qn, qr], axis=1) * Q_SCALE
            qt_ref[0, hh, r0 // tq, :, r0 % tq:r0 % tq + rows] = qh.T.astype(bf16)

        c = _rms(p1[:, Q_LORA:Q_LORA + KV_LORA], KV_LORA) * kvg_ref[...]
        ckv_ref[0, rs, :] = c
        cb = c.astype(bf16)
        ct_ref[0, r0 // tk, :, r0 % tk:r0 % tk + rows] = c.T.astype(bf16)
        kn = _dot(cb, wuk_ref[...])
        o = Q_LORA + KV_LORA
        krp = rope(_rms(p1[:, o:o + LANES], QK_ROPE) * gkr_ref[...])
        kr_ref[0, rs, :] = krp[:, :QK_ROPE]
        krb = krp.astype(bf16)
        for hh in range(N_HEADS_A):
            knh = _rms(kn[:, LANES * hh:LANES * (hh + 1)], QK_NOPE) * gkn_ref[...]
            k_ref[0, hh, rs, 0:LANES] = knh.astype(bf16)
            k_ref[0, hh, rs, LANES:QK_PAD] = krb

        sza_ref[0, rs, :] = _silu(p2[:, :D_ATT]).astype(bf16)
        min_ref[0, rs, :] = p2[:, D_ATT:D_ATT + D_MLS]
        szm_ref[0, rs, :] = _silu(p2[:, D_ATT + D_MLS:]).astype(bf16)


def _proj(x, cos, sin, wp, tm, tq, tk):
    nb, s, _ = x.shape
    grid = (nb, s // tm)
    full = lambda a: pl.BlockSpec(a.shape, lambda b, i: (0,) * a.ndim)
    tok = lambda w: pl.BlockSpec((1, tm, w), lambda b, i: (b, i, 0))
    head = pl.BlockSpec((1, N_HEADS_A, tm, QK_PAD), lambda b, i: (b, 0, i, 0))
    head_t = pl.BlockSpec((1, N_HEADS_A, tm // tq, QK_PAD, tq), lambda b, i: (b, 0, i, 0, 0))
    lat_t = pl.BlockSpec((1, tm // tk, KV_LORA, tk), lambda b, i: (b, i, 0, 0))
    tab = pl.BlockSpec((tm, LANES), lambda b, i: (i, 0))
    weights = (wp["norm_g"], wp["w1"], wp["w2"], wp["q_norm_g"], wp["w_uq"], wp["kv_norm_g"], wp["w_uk"],
               wp["g_qn"], wp["g_qr"], wp["g_kn"], wp["g_kr"])
    out_shape = (
        jax.ShapeDtypeStruct((nb, N_HEADS_A, s // tq, QK_PAD, tq), bf16),
        jax.ShapeDtypeStruct((nb, N_HEADS_A, s, QK_PAD), bf16),
        jax.ShapeDtypeStruct((nb, s, KV_LORA), f32),
        jax.ShapeDtypeStruct((nb, s // tk, KV_LORA, tk), bf16),
        jax.ShapeDtypeStruct((nb, s, QK_ROPE), f32),
        jax.ShapeDtypeStruct((nb, s, D_ATT), bf16),
        jax.ShapeDtypeStruct((nb, s, D_MLS), f32),
        jax.ShapeDtypeStruct((nb, s, D_MLS), bf16),
    )
    out_specs = (head_t, head, tok(KV_LORA), lat_t, tok(QK_ROPE), tok(D_ATT), tok(D_MLS), tok(D_MLS))
    return pl.pallas_call(
        _proj_kernel,
        grid=grid,
        in_specs=[tok(D_MODEL), tab, tab] + [full(w) for w in weights],
        out_specs=out_specs,
        out_shape=out_shape,
        compiler_params=pltpu.CompilerParams(dimension_semantics=("arbitrary", "arbitrary"),
                                             vmem_limit_bytes=VMEM_LIMIT),
        name="in_proj",
    )(x, cos, sin, *weights)


def _attn_kernel(qt_ref, k_ref, ct_ref, sza_ref, wuv_ref, x_ref, mm_ref, woa_ref, wom_ref, y_ref,
                 m_sc, l_sc, acc_sc, s_sc, mixa_sc, *, tq, tk):
    i = pl.program_id(1)
    ratio = tq // tk
    kv_idx = lax.broadcasted_iota(jnp.int32, (tk, tq), 0)
    q_idx = lax.broadcasted_iota(jnp.int32, (tk, tq), 1)

    m_sc[...] = jnp.full(m_sc.shape, -jnp.inf, f32)
    l_sc[...] = jnp.zeros(l_sc.shape, f32)
    acc_sc[...] = jnp.zeros(acc_sc.shape, f32)

    def steps(tiles):
        units = [(t, hh) for t in range(len(tiles)) for hh in range(N_HEADS_A)]

        def scores(u):
            t, hh = units[u]
            j, diag = tiles[t]
            start = pl.multiple_of(j * tk, tk)
            s = _dot(k_ref[0, hh, pl.ds(start, tk), :], qt_ref[0, hh, 0])
            s_sc[t % 2, hh] = s if diag is None else jnp.where(kv_idx + diag * tk <= q_idx, s, -jnp.inf)

        def update(u):
            t, hh = units[u]
            s = s_sc[t % 2, hh]
            m_old = m_sc[hh]
            m_new = jnp.maximum(m_old, jnp.max(s, axis=0, keepdims=True))
            alpha = jnp.exp2(m_old - m_new)
            p = jnp.exp2(s - m_new[0:1, :])
            l_sc[hh] = alpha * l_sc[hh] + jnp.sum(p, axis=0, keepdims=True)
            acc_sc[hh] = alpha[0:1, :] * acc_sc[hh] + _dot(ct_ref[0, tiles[t][0]], p.astype(bf16))
            m_sc[hh] = m_new

        for u in range(min(AHEAD, len(units))):
            scores(u)
        for u in range(len(units)):
            if u + AHEAD < len(units):
                scores(u + AHEAD)
            update(u)

    def body(jj, carry):
        steps([(ratio * jj + d, None) for d in range(ratio)])
        return carry

    lax.fori_loop(0, i, body, 0)
    steps([(ratio * i + d, d) for d in range(ratio)])

    for hh in range(N_HEADS_A):
        inv = 1.0 / l_sc[hh]
        ctx = (acc_sc[hh] * inv[0:1, :]).T.astype(bf16)
        att = _dot(ctx, wuv_ref[hh])
        sl = slice(V_HEAD * hh, V_HEAD * (hh + 1))
        mixa_sc[:, sl] = (att * sza_ref[0, :, sl].astype(f32)).astype(bf16)
    y_ref[0] = x_ref[0] + _dot(mixa_sc[...], woa_ref[...]) + _dot(mm_ref[0], wom_ref[...])


def _attn_prompt(qt, k, ct, sza, x, mixm, wp, tq, tk):
    nb, _, s, _ = k.shape
    assert tq % tk == 0
    assert qt.shape == (nb, N_HEADS_A, s // tq, QK_PAD, tq) and ct.shape == (nb, s // tk, KV_LORA, tk)
    full = lambda a: pl.BlockSpec(a.shape, lambda b, i: (0,) * a.ndim)
    tok = lambda w: pl.BlockSpec((1, tq, w), lambda b, i: (b, i, 0))
    return pl.pallas_call(
        functools.partial(_attn_kernel, tq=tq, tk=tk),
        grid=(nb, s // tq),
        in_specs=[
            pl.BlockSpec((1, N_HEADS_A, 1, QK_PAD, tq), lambda b, i: (b, 0, i, 0, 0)),
            pl.BlockSpec((1, N_HEADS_A, s, QK_PAD), lambda b, i: (b, 0, 0, 0)),
            pl.BlockSpec((1, s // tk, KV_LORA, tk), lambda b, i: (b, 0, 0, 0)),
            tok(D_ATT), full(wp["w_uv"]), tok(D_MODEL), tok(D_MLS), full(wp["w_out_a"]), full(wp["w_out_m"]),
        ],
        out_specs=tok(D_MODEL),
        out_shape=jax.ShapeDtypeStruct((nb, s, D_MODEL), f32),
        scratch_shapes=[pltpu.VMEM((N_HEADS_A, 8, tq), f32),
                        pltpu.VMEM((N_HEADS_A, 8, tq), f32),
                        pltpu.VMEM((N_HEADS_A, KV_LORA, tq), f32),
                        pltpu.VMEM((2, N_HEADS_A, tk, tq), f32),
                        pltpu.VMEM((tq, D_ATT), bf16)],
        compiler_params=pltpu.CompilerParams(dimension_semantics=("arbitrary", "arbitrary"),
                                             vmem_limit_bytes=VMEM_LIMIT),
        name="attn_prompt",
    )(qt, k, ct, sza, wp["w_uv"], x, mixm, wp["w_out_a"], wp["w_out_m"])


def _scan_lanes(x, op, fill, length):
    lane = lax.broadcasted_iota(jnp.int32, x.shape, 1) % length
    d = 1
    while d < length:
        x = op(x, jnp.where(lane >= d, pltpu.roll(x, d, 1), fill))
        d *= 2
    return x


def _mlstm_kernel(min_ref, szm_ref, cw_ref, cb_ref, wqk_ref, wv_ref, wg_ref, gb_ref, lng_ref, skip_ref, sel_ref,
                  mix_ref, cn_ref, m_ref, halo_ref,
                  xbuf, ca_sc, qkv_sc, ks_sc, cn_sc, m_sc, cols_sc, inter_sc, sqk_sc, *, ts, lc, ne):
    si = pl.program_id(1)
    ns = pl.num_programs(1)
    nc = ts // lc

    @pl.when(si == 0)
    def _():
        xbuf[:, 0:8, :] = jnp.zeros((ne, 8, D_MLS), f32)
        cn_sc[...] = jnp.zeros(cn_sc.shape, f32)
        m_sc[...] = jnp.zeros(m_sc.shape, f32)

    row = lax.broadcasted_iota(jnp.int32, (lc, lc), 0)
    col = lax.broadcasted_iota(jnp.int32, (lc, lc), 1)
    causal = col <= row
    ones_b = jnp.ones((lc, LANES), bf16)

    def projections(e):
        x = min_ref[e]
        xbuf[e, 8:8 + ts, :] = x
        y = cb_ref[...] + xbuf[e, 8:8 + ts, :] * cw_ref[3:4, :]
        for j in range(CONV_W - 1):
            y = y + xbuf[e, 5 + j:5 + j + ts, :] * cw_ref[j:j + 1, :]
        xbuf[e, 0:8, :] = xbuf[e, ts:ts + 8, :]
        ca = _silu(y)
        ca_sc[e] = ca
        cab = ca.astype(bf16)
        xb = x.astype(bf16)
        for hh in range(N_HEADS_M):
            sl = slice(DH_M * hh, DH_M * (hh + 1))
            qk = _dot(cab[:, sl], wqk_ref[hh])
            qkv_sc[e, :, sl] = qk[:, :DH_M].astype(bf16)
            kh = qk[:, DH_M:]
            qkv_sc[e, :, D_MLS + DH_M * hh:D_MLS + DH_M * (hh + 1)] = kh.astype(bf16)
            ks_sc[e, :, sl] = (kh * K_SCALE_M).astype(bf16)
            qkv_sc[e, :, 2 * D_MLS + DH_M * hh:2 * D_MLS + DH_M * (hh + 1)] = (
                _dot(xb[:, sl], wv_ref[hh]).astype(bf16))

    def gate_scans(e):
        g = _dot(qkv_sc[e], wg_ref[...]) + gb_ref[...]
        gt = g.T[0:8, :]
        head_row = lax.broadcasted_iota(jnp.int32, gt.shape, 0) < N_HEADS_M
        li_all = jnp.where(head_row, gt, 0.0)
        lf_all = jnp.where(head_row, jax.nn.log_sigmoid(pltpu.roll(gt, 4, 0)), 0.0)
        for c in range(nc):
            for hh in range(N_HEADS_M):
                cs = slice(lc * c, lc * (c + 1))
                sl = slice(DH_M * hh, DH_M * (hh + 1))
                sqk_sc[e, c, hh] = _dot_nt(qkv_sc[e, cs, sl], ks_sc[e, cs, sl])
        b_all = _scan_lanes(lf_all, jnp.add, 0.0, lc)
        u_all = li_all - b_all
        cm_all = _scan_lanes(u_all, jnp.maximum, -jnp.inf, lc)
        u_max = [jnp.max(u_all[:, lc * c:lc * (c + 1)], axis=1, keepdims=True) for c in range(nc)]
        lf_sum = [jnp.sum(lf_all[:, lc * c:lc * (c + 1)], axis=1, keepdims=True) for c in range(nc)]
        return b_all, u_all, cm_all, u_max, lf_sum

    def chunk_refs(e, c, hh):
        cs = slice(lc * c, lc * (c + 1))
        sl = slice(DH_M * hh, DH_M * (hh + 1))
        vc = qkv_sc[e, cs, 2 * D_MLS + DH_M * hh:2 * D_MLS + DH_M * (hh + 1)]
        v_aug = jnp.concatenate([vc, ones_b], axis=1)
        return cs, sl, qkv_sc[e, cs, sl], ks_sc[e, cs, sl], v_aug

    def state_pass(e, c, scans, m0):
        b_all, u_all, cm_all, u_max, lf_sum = scans
        cs = slice(lc * c, lc * (c + 1))
        u = u_all[:, cs]
        gmax = jnp.maximum(m0, cm_all[:, cs])
        mt = b_all[:, cs] + gmax
        g_last = jnp.maximum(m0, u_max[c])
        m_new = lf_sum[c] + g_last
        decay = jnp.exp(m0 - g_last)
        w_s = jnp.exp(u - g_last)
        r = jnp.concatenate([gmax, mt], axis=0)
        r_hi = r.astype(bf16).astype(f32)
        r_mid = (r - r_hi).astype(bf16).astype(f32)
        r_lo = r - r_hi - r_mid
        stack = jnp.concatenate([r_hi, r_mid, r_lo, jnp.zeros((lc - 48, lc), f32)], axis=0)
        cols_sc[e, c] = _dot(stack.T.astype(bf16), sel_ref[...])
        for hh in range(N_HEADS_M):
            _, _, qc, kc, v_aug = chunk_refs(e, c, hh)
            inter_sc[e, c, hh] = _dot(qc, cn_sc[e, hh].astype(bf16))
            ktw = (kc.astype(f32).T * w_s[hh:hh + 1, :]).astype(bf16)
            upd = _dot(ktw, v_aug)
            dec = jnp.broadcast_to(decay[hh:hh + 1, :], (DH_M, LANES))
            cn_sc[e, hh] = _rep_lanes(dec, 2) * cn_sc[e, hh] + upd
        return u, m_new

    def output_pass(e, c, u, m0):
        for hh in range(N_HEADS_M):
            cs, sl, _, _, v_aug = chunk_refs(e, c, hh)
            g_col = cols_sc[e, c, :, LANES * hh:LANES * (hh + 1)]
            mt_col = cols_sc[e, c, :, LANES * (N_HEADS_M + hh):LANES * (N_HEADS_M + hh + 1)]
            wm = jnp.where(causal, jnp.exp(u[hh:hh + 1, :] - g_col), 0.0)
            p = (sqk_sc[e, c, hh] * wm).astype(bf16)
            intra = _dot(p, v_aug)
            w_inter = jnp.exp(m0[hh:hh + 1, :] - g_col)
            tot = _rep_lanes(w_inter, 2) * inter_sc[e, c, hh] + intra
            num = tot[:, :DH_M]
            den = tot[:, DH_M:]
            hv = num / jnp.maximum(jnp.abs(den), jnp.exp(-mt_col))
            mu = jnp.mean(hv, axis=-1, keepdims=True)
            hc = hv - mu
            var = jnp.mean(hc * hc, axis=-1, keepdims=True)
            hn = hc * lax.rsqrt(var + LN_EPS) * lng_ref[:, sl]
            out = hn + skip_ref[:, sl] * ca_sc[e, cs, sl]
            mix_ref[e, cs, sl] = (out * szm_ref[e, cs, sl].astype(f32)).astype(bf16)

    seqs = range(ne)
    scans = []
    for e in seqs:
        projections(e)
        scans.append(gate_scans(e))
    m_cur = [m_sc[e] for e in seqs]
    staged = [[] for _ in seqs]
    for t in range(nc + 1 + SEQ_LAG * (ne - 1)):
        for e in seqs:
            c = t - SEQ_LAG * e
            if 0 <= c < nc:
                u_c, m_next = state_pass(e, c, scans[e], m_cur[e])
                staged[e].append((u_c, m_cur[e]))
                m_cur[e] = m_next
            if 1 <= c <= nc:
                output_pass(e, c - 1, *staged[e][c - 1])
    for e in seqs:
        m_sc[e] = m_cur[e]

    @pl.when(si == ns - 1)
    def _():
        cn_ref[...] = cn_sc[...]
        m_ref[...] = m_sc[...]
        halo_ref[...] = xbuf[:, 0:8, :]


def _column_selector():
    rows = lax.broadcasted_iota(jnp.int32, (LANES, 2 * N_HEADS_M * LANES), 0)
    block = lax.broadcasted_iota(jnp.int32, (LANES, 2 * N_HEADS_M * LANES), 1) // LANES
    src = 8 * (block // N_HEADS_M) + block % N_HEADS_M
    return ((rows < 48) & (rows % 16 == src)).astype(bf16)


def _mlstm_prompt(m_in, szm, wp, ts, lc):
    nb, s, _ = m_in.shape
    full = lambda a: pl.BlockSpec(a.shape, lambda b, i: (0,) * a.ndim)
    ne = 2 if nb % 2 == 0 else 1
    tok = pl.BlockSpec((ne, ts, D_MLS), lambda b, i: (b, i, 0))
    assert lc == LANES, "one chunk spans exactly one vreg of lanes"
    weights = (wp["conv_w"], wp["conv_b"], wp["w_qk_m"], wp["w_v_m"], wp["w_gate"], wp["gate_b"],
               wp["mh_norm_g"], wp["skip"], _column_selector())
    nc = ts // lc
    return pl.pallas_call(
        functools.partial(_mlstm_kernel, ts=ts, lc=lc, ne=ne),
        grid=(nb // ne, s // ts),
        in_specs=[tok, tok] + [full(w) for w in weights],
        out_specs=(tok,
                   pl.BlockSpec((ne, N_HEADS_M, DH_M, 2 * DH_M), lambda b, i: (b, 0, 0, 0)),
                   pl.BlockSpec((ne, 8, LANES), lambda b, i: (b, 0, 0)),
                   pl.BlockSpec((ne, 8, D_MLS), lambda b, i: (b, 0, 0))),
        out_shape=(jax.ShapeDtypeStruct((nb, s, D_MLS), bf16),
                   jax.ShapeDtypeStruct((nb, N_HEADS_M, DH_M, 2 * DH_M), f32),
                   jax.ShapeDtypeStruct((nb, 8, LANES), f32),
                   jax.ShapeDtypeStruct((nb, 8, D_MLS), f32)),
        scratch_shapes=[pltpu.VMEM((ne, ts + 8, D_MLS), f32),
                        pltpu.VMEM((ne, ts, D_MLS), f32),
                        pltpu.VMEM((ne, ts, 3 * D_MLS), bf16),
                        pltpu.VMEM((ne, ts, D_MLS), bf16),
                        pltpu.VMEM((ne, N_HEADS_M, DH_M, 2 * DH_M), f32),
                        pltpu.VMEM((ne, 8, LANES), f32),
                        pltpu.VMEM((ne, nc, lc, 2 * N_HEADS_M * LANES), f32),
                        pltpu.VMEM((ne, nc, N_HEADS_M, lc, 2 * DH_M), f32),
                        pltpu.VMEM((ne, nc, N_HEADS_M, lc, lc), f32)],
        compiler_params=pltpu.CompilerParams(dimension_semantics=("arbitrary", "arbitrary"),
                                             vmem_limit_bytes=VMEM_LIMIT),
        name="mlstm_prompt",
    )(m_in, szm, *weights)


def _sattn_kernel(pt_ref, q_ref, knew_ref, cnew_ref, wukt_ref, gkn_ref, ckv_hbm, krt_hbm, ctx_ref,
                  cbuf0, cbuf1, kbuf0, kbuf1, sem, cb_sc, knt_sc, lhs_sc, *, layer, pb, sub):
    b = pl.program_id(0)
    nbatch = pl.num_programs(0)
    npages = pt_ref.shape[1]
    nblk = npages // pb
    tb = pb * PAGE_SIZE
    nsub = tb // sub
    sub_pages = sub // PAGE_SIZE
    cbufs = (cbuf0, cbuf1)
    kbufs = (kbuf0, kbuf1)

    def part_copies(bb, blk, slot, si):
        latent, ropek = [], []
        for p in range(sub_pages * si, sub_pages * (si + 1)):
            page = pt_ref[bb, blk * pb + p]
            rows = pl.ds(p * PAGE_SIZE, PAGE_SIZE)
            latent.append(pltpu.make_async_copy(ckv_hbm.at[layer, page], cbufs[slot].at[rows], sem.at[slot, si, 0]))
            ropek.append(pltpu.make_async_copy(krt_hbm.at[layer, page], kbufs[slot].at[p], sem.at[slot, si, 1]))
        return latent + ropek

    def start_part(bb, blk, slot, si):
        for cp in part_copies(bb, blk, slot, si):
            cp.start()

    def wait_part(bb, blk, slot, si):
        for cp in part_copies(bb, blk, slot, si):
            cp.wait()

    def two_ahead(blk):
        wraps = blk + 2 >= nblk
        return jnp.where(wraps, jnp.minimum(b + 1, nbatch - 1), b), jnp.where(wraps, blk + 2 - nblk, blk + 2)

    @pl.when(b == 0)
    def _():
        for slot in range(2):
            for si in range(nsub):
                start_part(0, slot, slot, si)

    q = q_ref[0]
    qg = q[:, :QK_NOPE].astype(f32) * gkn_ref[...]
    rowq = lax.broadcasted_iota(jnp.int32, (16, QK_NOPE), 0)
    qabs = jnp.zeros((16, KV_LORA), f32)
    for hh in range(N_HEADS_A):
        lhs = jnp.where(rowq == hh, jnp.broadcast_to(qg[hh:hh + 1, :], (16, QK_NOPE)), 0.0).astype(bf16)
        qabs = qabs + _dot(lhs, wukt_ref[QK_NOPE * hh:QK_NOPE * (hh + 1), :])
    nk = N_HEADS_A * QK_NOPE
    lhs_sc[0:nk, :] = wukt_ref[...]
    lhs_sc[nk:nk + 16, :] = qabs.astype(bf16)
    qr16 = jnp.concatenate([q[:, QK_NOPE:QK_NOPE + QK_ROPE],
                            jnp.zeros((16 - N_HEADS_A, QK_ROPE), bf16)], axis=0)

    row16 = lax.broadcasted_iota(jnp.int32, (16, sub), 0)

    def matmuls(slot, si):
        cb = cbufs[slot][sub * si:sub * (si + 1), :].astype(bf16)
        cb_sc[si % 2] = cb
        knt_sc[si % 2] = _dot_nt(lhs_sc[...], cb)
        srope = jnp.concatenate(
            [_dot(qr16, kbufs[slot][sub_pages * si + p].astype(bf16)) for p in range(sub_pages)],
            axis=1)
        return srope

    def softmax_update(si, srope, carry):
        num = knt_sc[si % 2, nk:nk + 16, :]
        m, l, acc = carry
        rnorm = jnp.zeros((16, sub), f32)
        for hh in range(N_HEADS_A):
            kh = knt_sc[si % 2, QK_NOPE * hh:QK_NOPE * (hh + 1), :]
            ss = jnp.sum(kh * kh, axis=0, keepdims=True)
            rnorm = jnp.where(row16 == hh, lax.rsqrt(ss * (1.0 / QK_NOPE) + EPS), rnorm)
        s = num * rnorm + srope
        m_new = jnp.maximum(m, jnp.max(s, axis=1, keepdims=True))
        alpha = jnp.exp2(m - m_new)
        p = jnp.exp2(s - _rep_lanes(m_new, sub // LANES))
        l = alpha * l + jnp.sum(p, axis=1, keepdims=True)
        acc = _rep_lanes(alpha, KV_LORA // LANES) * acc + _dot(p.astype(bf16), cb_sc[si % 2])
        return m_new, l, acc

    def compute(slot, blk, carry):
        nxt_b, nxt_blk = two_ahead(blk)
        for si in range(nsub):
            wait_part(b, blk, slot, si)

        def stage(si):
            staged = matmuls(slot, si)
            start_part(nxt_b, nxt_blk, slot, si)
            return staged

        staged = stage(0)
        for si in range(nsub):
            nxt = stage(si + 1) if si + 1 < nsub else None
            carry = softmax_update(si, staged, carry)
            staged = nxt
        return carry

    def pair(i, carry):
        return compute(1, 2 * i + 1, compute(0, 2 * i, carry))

    init = (jnp.full((16, LANES), -jnp.inf, f32), jnp.zeros((16, LANES), f32), jnp.zeros((16, KV_LORA), f32))
    m_old, l_old, acc_old = lax.fori_loop(0, nblk // 2, pair, init)

    @pl.when(b == nbatch - 1)
    def _():
        for slot in range(2):
            for si in range(nsub):
                wait_part(b, slot, slot, si)

    kn = knew_ref[0].astype(f32)
    s_new = jnp.sum(q.astype(f32) * kn, axis=1, keepdims=True)
    s_new = jnp.concatenate([jnp.broadcast_to(s_new, (N_HEADS_A, LANES)),
                             jnp.zeros((16 - N_HEADS_A, LANES), f32)], axis=0)
    m_new = jnp.maximum(m_old, s_new)
    alpha = jnp.exp2(m_old - m_new)
    p_new = jnp.exp2(s_new - m_new)
    l = alpha * l_old + p_new
    acc = _rep_lanes(alpha, 2) * acc_old + _rep_lanes(p_new, 2) * cnew_ref[0]
    ctx = acc * _rep_lanes(1.0 / l, 2)
    ctx_ref[0] = ctx[0:N_HEADS_A, :]


def _attn_sample(page_table, q_s, k_s, c_s, wp, cache_ckv, cache_krope_t, layer, pb, sub):
    nbd = q_s.shape[0]
    tb = pb * PAGE_SIZE
    full = lambda a: pl.BlockSpec(a.shape, lambda b, pt: (0,) * a.ndim)
    grid_spec = pltpu.PrefetchScalarGridSpec(
        num_scalar_prefetch=1,
        grid=(nbd,),
        in_specs=[pl.BlockSpec((1, N_HEADS_A, QK_PAD), lambda b, pt: (b, 0, 0)),
                  pl.BlockSpec((1, N_HEADS_A, QK_PAD), lambda b, pt: (b, 0, 0)),
                  pl.BlockSpec((1, 1, KV_LORA), lambda b, pt: (b, 0, 0)),
                  full(wp["w_uk_t"]), full(wp["g_kn"]),
                  pl.BlockSpec(memory_space=pl.ANY),
                  pl.BlockSpec(memory_space=pl.ANY)],
        out_specs=pl.BlockSpec((1, N_HEADS_A, KV_LORA), lambda b, pt: (b, 0, 0)),
        scratch_shapes=[pltpu.VMEM((tb, KV_LORA), f32),
                        pltpu.VMEM((tb, KV_LORA), f32),
                        pltpu.VMEM((pb, QK_ROPE, PAGE_SIZE), f32),
                        pltpu.VMEM((pb, QK_ROPE, PAGE_SIZE), f32),
                        pltpu.SemaphoreType.DMA((2, tb // sub, 2)),
                        pltpu.VMEM((2, sub, KV_LORA), bf16),
                        pltpu.VMEM((2, N_HEADS_A * QK_NOPE + 16, sub), f32),
                        pltpu.VMEM((N_HEADS_A * QK_NOPE + 16, KV_LORA), bf16)],
    )
    return pl.pallas_call(
        functools.partial(_sattn_kernel, layer=layer, pb=pb, sub=sub),
        grid_spec=grid_spec,
        out_shape=jax.ShapeDtypeStruct((nbd, N_HEADS_A, KV_LORA), f32),
        compiler_params=pltpu.CompilerParams(dimension_semantics=("arbitrary",), vmem_limit_bytes=VMEM_LIMIT),
        name="attn_sample",
    )(page_table, q_s, k_s, c_s, wp["w_uk_t"], wp["g_kn"], cache_ckv, cache_krope_t)


def _stail_kernel(x_ref, ctx_ref, sza_ref, min_ref, szm_ref, conv_ref, c0_ref, n0_ref, m0_ref,
                  wuv_ref, cw_ref, cb_ref, wqk_ref, wv_ref, wg_ref, gb_ref, lng_ref, skip_ref, woa_ref, wom_ref,
                  y_ref, c1_ref, n1_ref, m1_ref, conv1_ref, *, bb):
    mixa = []
    for hh in range(N_HEADS_A):
        att = _dot(ctx_ref[hh].astype(bf16), wuv_ref[hh])
        sl = slice(V_HEAD * hh, V_HEAD * (hh + 1))
        mixa.append((att * sza_ref[:, sl].astype(f32)).astype(bf16))
    mixa = jnp.concatenate(mixa, axis=1)

    x = min_ref[...]
    y = cb_ref[...] + x * cw_ref[3:4, :]
    for j in range(CONV_W - 1):
        y = y + conv_ref[j] * cw_ref[j:j + 1, :]
    for j in range(CONV_W - 2):
        conv1_ref[j] = conv_ref[j + 1]
    conv1_ref[CONV_W - 2] = x
    ca = _silu(y)
    cab = ca.astype(bf16)
    xb = x.astype(bf16)
    qs, ks, vs = [], [], []
    for hh in range(N_HEADS_M):
        sl = slice(DH_M * hh, DH_M * (hh + 1))
        qk = _dot(cab[:, sl], wqk_ref[hh])
        qs.append(qk[:, :DH_M])
        ks.append(qk[:, DH_M:])
        vs.append(_dot(xb[:, sl], wv_ref[hh]))
    qkv = jnp.concatenate(qs + ks + vs, axis=1).astype(bf16)
    g = _dot(qkv, wg_ref[...]) + gb_ref[...]
    li = g
    lf = jax.nn.log_sigmoid(pltpu.roll(g, LANES - N_HEADS_M, 1))
    m0 = m0_ref[...]
    a = lf + m0
    mt = jnp.maximum(a, li)
    w_inter = jnp.exp(a - mt)
    w_new = jnp.exp(li - mt)
    emt = jnp.exp(-mt)
    m1_ref[...] = mt

    rowi = lax.broadcasted_iota(jnp.int32, (bb, DH_M), 0)
    rowp = lax.broadcasted_iota(jnp.int32, (LANES, DH_M), 0)
    zpad = jnp.zeros((LANES - bb, DH_M), f32)
    mixm = []
    for hh in range(N_HEADS_M):
        sl = slice(DH_M * hh, DH_M * (hh + 1))
        col = lambda z: jnp.broadcast_to(z[:, hh:hh + 1], (bb, DH_M))
        wi, wn, em = col(w_inter), col(w_new), col(emt)
        qh = qs[hh]
        kh = ks[hh] * K_SCALE_M
        vh = vs[hh]
        qb_ = qh.astype(bf16)
        kw = (kh * wn)
        kwt = jnp.concatenate([kw, zpad], axis=0).T.astype(bf16)
        vb_ = vh.astype(bf16)
        vpad = jnp.concatenate([vh, zpad], axis=0)
        inter = jnp.zeros((bb, DH_M), f32)
        for r in range(bb):
            c0 = c0_ref[r, hh]
            inter = inter + jnp.where(rowi == r, _dot(qb_, c0.astype(bf16)), 0.0)
            upd = _dot(kwt, jnp.where(rowp == r, vpad, 0.0).astype(bf16))
            dec = jnp.broadcast_to(wi[r:r + 1, :], (DH_M, DH_M))
            c1_ref[r, hh] = dec * c0 + upd
        n0 = n0_ref[hh]
        n1_ref[hh] = wi * n0 + kw
        qk_dot = jnp.sum(qb_.astype(f32) * kh.astype(bf16).astype(f32), axis=1, keepdims=True) * wn
        num = wi * inter + qk_dot * vb_.astype(f32)
        den = wi * jnp.sum(qb_.astype(f32) * n0.astype(bf16).astype(f32), axis=1, keepdims=True) + qk_dot
        hv = num / jnp.maximum(jnp.abs(den), em)
        mu = jnp.mean(hv, axis=-1, keepdims=True)
        hc = hv - mu
        var = jnp.mean(hc * hc, axis=-1, keepdims=True)
        hn = hc * lax.rsqrt(var + LN_EPS) * lng_ref[:, sl]
        out = hn + skip_ref[:, sl] * ca[:, sl]
        mixm.append((out * szm_ref[:, sl].astype(f32)).astype(bf16))
    mixm = jnp.concatenate(mixm, axis=1)
    y_ref[...] = x_ref[...] + _dot(mixa, woa_ref[...]) + _dot(mixm, wom_ref[...])


def _sample_tail(x_s, ctx, sza, m_in, szm, conv0, c0, n0, m0, wp, bb):
    nbd = x_s.shape[0]
    full = lambda a: pl.BlockSpec(a.shape, lambda i: (0,) * a.ndim)
    row = lambda w: pl.BlockSpec((bb, w), lambda i: (i, 0))
    weights = (wp["w_uv"], wp["conv_w"], wp["conv_b"], wp["w_qk_m"], wp["w_v_m"], wp["w_gate"], wp["gate_b"],
               wp["mh_norm_g"], wp["skip"], wp["w_out_a"], wp["w_out_m"])
    conv_spec = pl.BlockSpec((CONV_W - 1, bb, D_MLS), lambda i: (0, i, 0))
    c_spec = pl.BlockSpec((bb, N_HEADS_M, DH_M, DH_M), lambda i: (i, 0, 0, 0))
    n_spec = pl.BlockSpec((N_HEADS_M, bb, DH_M), lambda i: (0, i, 0))
    return pl.pallas_call(
        functools.partial(_stail_kernel, bb=bb),
        grid=(nbd // bb,),
        in_specs=[row(D_MODEL),
                  pl.BlockSpec((N_HEADS_A, bb, KV_LORA), lambda i: (0, i, 0)),
                  row(D_ATT), row(D_MLS), row(D_MLS), conv_spec, c_spec, n_spec, row(LANES)]
                 + [full(w) for w in weights],
        out_specs=(row(D_MODEL), c_spec, n_spec, row(LANES), conv_spec),
        out_shape=(jax.ShapeDtypeStruct((nbd, D_MODEL), f32),
                   jax.ShapeDtypeStruct(c0.shape, f32),
                   jax.ShapeDtypeStruct(n0.shape, f32),
                   jax.ShapeDtypeStruct((nbd, LANES), f32),
                   jax.ShapeDtypeStruct(conv0.shape, f32)),
        compiler_params=pltpu.CompilerParams(dimension_semantics=("arbitrary",), vmem_limit_bytes=VMEM_LIMIT),
        name="sample_tail",
    )(x_s, ctx, sza, m_in, szm, conv0, c0, n0, m0, *weights)


def _prep_weights(l, norm_g, w_in, q_norm_g, w_uq, kv_norm_g, w_uk, w_uv, g_qn, g_qr, g_kn, g_kr,
                  conv_w, conv_b, w_q_m, w_k_m, w_v_m, w_gate, b_i, b_f, mh_norm_g, skip, w_out):
    wi = w_in[l]
    o1 = Q_LORA + KV_LORA + QK_ROPE
    w1 = jnp.concatenate([wi[:, :o1], jnp.zeros((D_MODEL, LANES - QK_ROPE), f32)], axis=1)
    w2 = wi[:, o1:]
    wq = w_uq[l].reshape(Q_LORA, N_HEADS_A, QK_NOPE + QK_ROPE)
    wq_rope = jnp.concatenate([wq[:, :, QK_NOPE:], jnp.zeros((Q_LORA, N_HEADS_A, LANES - QK_ROPE), f32)], axis=2)
    wuq = jnp.concatenate([wq[:, :, :QK_NOPE].reshape(Q_LORA, -1), wq_rope.reshape(Q_LORA, -1)], axis=1)
    pad_rope = lambda g: jnp.concatenate([g, jnp.zeros((LANES - QK_ROPE,), f32)])[None, :]
    wuk = w_uk[l].reshape(KV_LORA, N_HEADS_A * QK_NOPE)
    wg = jnp.concatenate([w_gate[l], jnp.zeros((3 * D_MLS, LANES - 2 * N_HEADS_M), f32)], axis=1)
    gate_b = jnp.concatenate([b_i[l], b_f[l], jnp.zeros((LANES - 2 * N_HEADS_M,), f32)])[None, :]
    return {
        "norm_g": norm_g[l][None, :],
        "w1": w1.astype(bf16),
        "w2": w2.astype(bf16),
        "q_norm_g": q_norm_g[l][None, :],
        "w_uq": wuq.astype(bf16),
        "kv_norm_g": kv_norm_g[l][None, :],
        "w_uk": wuk.astype(bf16),
        "w_uk_t": wuk.T.astype(bf16),
        "w_uv": jnp.transpose(w_uv[l], (1, 0, 2)).astype(bf16),
        "g_qn": g_qn[l][None, :],
        "g_qr": pad_rope(g_qr[l]),
        "g_kn": g_kn[l][None, :],
        "g_kr": pad_rope(g_kr[l]),
        "conv_w": conv_w[l],
        "conv_b": conv_b[l][None, :],
        "w_qk_m": jnp.concatenate([w_q_m[l], w_k_m[l]], axis=2).astype(bf16),
        "w_v_m": w_v_m[l].astype(bf16),
        "w_gate": wg.astype(bf16),
        "gate_b": gate_b,
        "mh_norm_g": mh_norm_g[l][None, :],
        "skip": skip[l][None, :],
        "w_out_a": w_out[l][:D_ATT].astype(bf16),
        "w_out_m": w_out[l][D_ATT:].astype(bf16),
    }


def _pick(n, candidates):
    for c in candidates:
        if n % c == 0:
            return c
    raise ValueError(f"no tile size for extent {n}")


def kernel(x_prompt, x_sample, cache_ckv, cache_krope, state_C, state_n, state_m, state_conv, page_table,
           norm_g, w_in, q_norm_g, w_uq, kv_norm_g, w_uk, w_uv, g_qn, g_qr, g_kn, g_kr,
           conv_w, conv_b, w_q_m, w_k_m, w_v_m, w_gate, b_i, b_f, mh_norm_g, skip, w_out):
    nb, s, _ = x_prompt.shape
    nbd, sd, _ = x_sample.shape
    depth = norm_g.shape[0]
    assert sd == 1, "sample path handles one new token per sequence"
    npages = page_table.shape[1]
    past_len = npages * PAGE_SIZE

    cos_p, sin_p = _rope_tables(jnp.arange(s))
    cos_s, sin_s = _rope_tables(jnp.full((nbd,), past_len, jnp.int32))

    tm_p = _pick(s, (1024, 512, 256, 128))
    tk_att = _pick(s, (512, 256, 128))
    tq_att = _pick(tm_p, (tk_att,))
    ts_m = _pick(s, (512, 256, 128))
    assert npages % 2 == 0, "sample attention double-buffers an even number of page blocks"
    pb = _pick(npages // 2, (32, 16, 8, 4, 2, 1))
    sub = _pick(pb * PAGE_SIZE, (2048, 1024, 512, 256, 128))
    bb = _pick(nbd, (16,))

    cache_krope_t = jnp.swapaxes(cache_krope, 2, 3)

    yp = x_prompt
    ys = x_sample.reshape(1, nbd, D_MODEL)
    outs_p, outs_s = [], []
    for l in range(depth):
        wp = _prep_weights(l, norm_g, w_in, q_norm_g, w_uq, kv_norm_g, w_uk, w_uv, g_qn, g_qr, g_kn, g_kr,
                           conv_w, conv_b, w_q_m, w_k_m, w_v_m, w_gate, b_i, b_f, mh_norm_g, skip, w_out)
        qt, k, ckv, ct, kr, sza, m_in, szm = _proj(yp, cos_p, sin_p, wp, tm_p, tq_att, tk_att)
        mixm, cn, mm, halo = _mlstm_prompt(m_in, szm, wp, ts_m, LANES)
        yp = _attn_prompt(qt, k, ct, sza, yp, mixm, wp, tq_att, tk_att)
        outs_p.append((ckv, kr, cn[..., :DH_M], cn[..., DH_M], mm[:, :N_HEADS_M, 0],
                       halo[:, 8 - (CONV_W - 1):, :]))
        qt_s, k_s, ckv_s, _, kr_s, sza_s, m_in_s, szm_s = _proj(ys, cos_s, sin_s, wp, nbd, nbd, nbd)
        ctx = _attn_sample(page_table, jnp.transpose(qt_s[0, :, 0], (2, 0, 1)), jnp.transpose(k_s[0], (1, 0, 2)),
                           ckv_s.reshape(nbd, 1, KV_LORA), wp, cache_ckv, cache_krope_t, l, pb, sub)
        m0 = jnp.concatenate([state_m[l], jnp.zeros((nbd, LANES - N_HEADS_M), f32)], axis=1)
        to_lead = lambda a: jnp.transpose(a, (1, 0, 2))
        y_s, c1, n1, m1, conv1 = _sample_tail(ys[0], to_lead(ctx), sza_s[0], m_in_s[0], szm_s[0],
                                              to_lead(state_conv[l]), state_C[l], to_lead(state_n[l]), m0, wp, bb)
        ys = y_s.reshape(1, nbd, D_MODEL)
        outs_s.append((ckv_s.reshape(nbd, 1, KV_LORA), kr_s.reshape(nbd, 1, QK_ROPE), c1, to_lead(n1),
                       m1[:, :N_HEADS_M], to_lead(conv1)))
    stk = lambda outs, i: jnp.stack([o[i] for o in outs], axis=0)
    return (yp, ys.reshape(nbd, 1, D_MODEL),
            stk(outs_p, 0), stk(outs_p, 1), stk(outs_p, 2), stk(outs_p, 3), stk(outs_p, 4), stk(outs_p, 5),
            stk(outs_s, 0), stk(outs_s, 1), stk(outs_s, 2), stk(outs_s, 3), stk(outs_s, 4), stk(outs_s, 5))
```
